```python
import math
import jax, jax.numpy as jnp
from jax import lax
import numpy as np

D_MODEL = 2048
BATCH = 2
SEQ = 16384
DEPTH = 1

CHUNK = 64
Q_BLOCK = 128
ROPE_THETA = 10000.0
NORM_EPS = 1e-6

N_HEADS = 16
N_KV_HEADS = 4
HEAD_DIM = 128
ATTN_WIDTH = N_HEADS * HEAD_DIM
KV_WIDTH = N_KV_HEADS * HEAD_DIM
TOPK_MAX = 256
N_IDX_HEADS = 16
IDX_DIM = 64

SSM_EXPAND = 2
SSM_WIDTH = SSM_EXPAND * D_MODEL
SSM_HEAD_DIM = 64
SSM_HEADS = SSM_WIDTH // SSM_HEAD_DIM
SSM_GROUPS = 8
SSM_STATE = 128
CONV_WIDTH = 4
CONV_CH = SSM_WIDTH + 2 * SSM_GROUPS * SSM_STATE

D_FF = -(-8 * D_MODEL // (3 * 256)) * 256

IN_SIZES = (ATTN_WIDTH, KV_WIDTH, KV_WIDTH, N_IDX_HEADS * IDX_DIM, IDX_DIM, N_IDX_HEADS, SSM_WIDTH, CONV_CH, SSM_HEADS)
IN_COLS = sum(IN_SIZES)

kernel_name = 'hybrid_dsa_ssd_block'


def rms_norm(x, g):
    xf = x.astype(jnp.float32)
    y = xf * lax.rsqrt(jnp.mean(xf * xf, axis=-1, keepdims=True) + NORM_EPS)
    return (y * g.astype(jnp.float32)).astype(x.dtype)


def rope_tables(seq, dim, dtype):
    pos = jnp.arange(seq, dtype=jnp.float32)
    inv = ROPE_THETA ** (-jnp.arange(0, dim, 2, dtype=jnp.float32) / dim)
    ang = pos[:, None] * inv[None, :]
    return jnp.cos(ang).astype(dtype), jnp.sin(ang).astype(dtype)


def apply_rope(x, cos, sin):
    x1, x2 = jnp.split(x, 2, axis=-1)
    c = cos[:, None, :]
    s = sin[:, None, :]
    return jnp.concatenate([x1 * c - x2 * s, x2 * c + x1 * s], axis=-1)


def dsa_attention(q, k, v, q_idx, k_idx, w_idx):
    bsz, seq = q.shape[0], q.shape[1]
    topk = min(TOPK_MAX, seq // 4)
    nblk = seq // Q_BLOCK
    grp = N_HEADS // N_KV_HEADS
    key_pos = jnp.arange(seq, dtype=jnp.int32)

    def to_blocks(t):
        return t.reshape(bsz, nblk, Q_BLOCK, *t.shape[2:]).swapaxes(0, 1)

    def block_fn(args):
        qb, qib, wb, start = args
        qpos = start + jnp.arange(Q_BLOCK, dtype=jnp.int32)
        limit = (qpos // CHUNK + 1) * CHUNK
        visible = key_pos[None, :] < limit[:, None]
        dots = jnp.einsum('bqhd,bsd->bqhs', qib, k_idx)
        iscore = jnp.einsum('bqh,bqhs->bqs', wb, jax.nn.relu(dots)).astype(jnp.float32)
        iscore = jnp.where(visible[None], iscore, -jnp.inf)
        _, sel = lax.top_k(iscore, topk)
        sel_ok = sel < limit[None, :, None]
        kg = jax.vmap(lambda kk, ii: kk[ii])(k, sel)
        vg = jax.vmap(lambda vv, ii: vv[ii])(v, sel)
        qg = qb.reshape(bsz, Q_BLOCK, N_KV_HEADS, grp, HEAD_DIM)
        logits = jnp.einsum('bqhgd,bqkhd->bqhgk', qg, kg).astype(jnp.float32) * (HEAD_DIM ** -0.5)
        logits = jnp.where(sel_ok[:, :, None, None, :], logits, -jnp.inf)
        p = jax.nn.softmax(logits, axis=-1).astype(v.dtype)
        o = jnp.einsum('bqhgk,bqkhd->bqhgd', p, vg)
        return o.reshape(bsz, Q_BLOCK, ATTN_WIDTH)

    starts = jnp.arange(nblk, dtype=jnp.int32) * Q_BLOCK
    out = lax.map(block_fn, (to_blocks(q), to_blocks(q_idx), to_blocks(w_idx), starts))
    return out.swapaxes(0, 1).reshape(bsz, seq, ATTN_WIDTH)


def causal_dwconv(x, w, b):
    out = lax.conv_general_dilated(x, w[:, None, :], window_strides=(1,), padding=[(CONV_WIDTH - 1, 0)],
                                   dimension_numbers=('NWC', 'WIO', 'NWC'), feature_group_count=x.shape[-1])
    return out + b


def ssd_scan(xs, dt, a_head, bm, cm):
    bsz, seq = xs.shape[0], xs.shape[1]
    nc = seq // CHUNK
    hg = SSM_HEADS // SSM_GROUPS
    f32 = jnp.float32
    xdt = (xs.astype(f32) * dt[..., None]).reshape(bsz, nc, CHUNK, SSM_GROUPS, hg, SSM_HEAD_DIM)
    a = (dt * a_head).reshape(bsz, nc, CHUNK, SSM_GROUPS, hg)
    bf = bm.astype(f32).reshape(bsz, nc, CHUNK, SSM_GROUPS, SSM_STATE)
    cf = cm.astype(f32).reshape(bsz, nc, CHUNK, SSM_GROUPS, SSM_STATE)
    seq_first = lambda t: jnp.moveaxis(t, 1, 0)
    causal = jnp.tril(jnp.ones((CHUNK, CHUNK), dtype=bool))

    def step(h, inp):
        xc, ac, bc, cc = inp
        acum = jnp.cumsum(ac, axis=1)
        diff = acum[:, :, None] - acum[:, None, :]
        decay = jnp.exp(jnp.where(causal[None, :, :, None, None], diff, -jnp.inf))
        cb = jnp.einsum('btgn,bsgn->btsg', cc, bc)
        y = jnp.einsum('btsg,btsgh,bsghp->btghp', cb, decay, xc)
        y = y + jnp.einsum('btgn,bghpn,btgh->btghp', cc, h, jnp.exp(acum))
        tail = jnp.exp(acum[:, -1:] - acum)
        h = h * jnp.exp(acum[:, -1])[..., None, None] + jnp.einsum('bsgn,bsgh,bsghp->bghpn', bc, tail, xc)
        return h, y

    h0 = jnp.zeros((bsz, SSM_GROUPS, hg, SSM_HEAD_DIM, SSM_STATE), f32)
    _, ys = lax.scan(step, h0, (seq_first(xdt), seq_first(a), seq_first(bf), seq_first(cf)))
    return jnp.moveaxis(ys, 0, 1).reshape(bsz, seq, SSM_HEADS, SSM_HEAD_DIM)


def mamba_branch(xbc_raw, z, dt_raw, conv_w, conv_b, dt_bias, a_log, d_skip, g_norm):
    bsz, seq = z.shape[0], z.shape[1]
    f32 = jnp.float32
    xbc = jax.nn.silu(causal_dwconv(xbc_raw, conv_w, conv_b))
    xs, bm, cm = jnp.split(xbc, [SSM_WIDTH, SSM_WIDTH + SSM_GROUPS * SSM_STATE], axis=-1)
    xs = xs.reshape(bsz, seq, SSM_HEADS, SSM_HEAD_DIM)
    bm = bm.reshape(bsz, seq, SSM_GROUPS, SSM_STATE)
    cm = cm.reshape(bsz, seq, SSM_GROUPS, SSM_STATE)
    dt = jax.nn.softplus(dt_raw.astype(f32) + dt_bias.astype(f32))
    a_head = -jnp.exp(a_log.astype(f32))
    y = ssd_scan(xs, dt, a_head, bm, cm) + d_skip.astype(f32)[:, None] * xs.astype(f32)
    y = y.reshape(bsz, seq, SSM_WIDTH) * jax.nn.silu(z.astype(f32))
    yg = y.reshape(bsz, seq, SSM_GROUPS, SSM_WIDTH // SSM_GROUPS)
    yg = yg * lax.rsqrt(jnp.mean(yg * yg, axis=-1, keepdims=True) + NORM_EPS)
    return (yg.reshape(bsz, seq, SSM_WIDTH) * g_norm.astype(f32)).astype(z.dtype)


def hybrid_mixer(u, w_in, w_gate, w_attn_branch, w_ssm_branch, w_out, conv_w, conv_b, dt_bias, a_log, d_skip, g_ssm_norm):
    bsz, seq = u.shape[0], u.shape[1]
    proj = u @ w_in
    cuts = list(np.cumsum(IN_SIZES)[:-1])
    q, k, v, q_idx, k_idx, w_idx, z, xbc, dt_raw = jnp.split(proj, cuts, axis=-1)
    cos_a, sin_a = rope_tables(seq, HEAD_DIM, u.dtype)
    cos_i, sin_i = rope_tables(seq, IDX_DIM, u.dtype)
    q = apply_rope(q.reshape(bsz, seq, N_HEADS, HEAD_DIM), cos_a, sin_a)
    k = apply_rope(k.reshape(bsz, seq, N_KV_HEADS, HEAD_DIM), cos_a, sin_a)
    v = v.reshape(bsz, seq, N_KV_HEADS, HEAD_DIM)
    q_idx = apply_rope(q_idx.reshape(bsz, seq, N_IDX_HEADS, IDX_DIM), cos_i, sin_i)
    k_idx = apply_rope(k_idx[:, :, None, :], cos_i, sin_i)[:, :, 0]
    y_attn = dsa_attention(q, k, v, q_idx, k_idx, w_idx)
    y_ssm = mamba_branch(xbc, z, dt_raw, conv_w, conv_b, dt_bias, a_log, d_skip, g_ssm_norm)
    gates = jax.nn.sigmoid(u @ w_gate).reshape(bsz, seq, 2, D_MODEL)
    merged = gates[:, :, 0] * (y_attn @ w_attn_branch) + gates[:, :, 1] * (y_ssm @ w_ssm_branch)
    return merged @ w_out


def setup_inputs(seed: int = 0) -> dict:
    key = jax.random.key(seed)
    ks = jax.random.split(key, 20)
    f32 = jnp.float32
    L = DEPTH

    def nrm(k, shape, fan_in):
        return jax.random.normal(k, shape, f32) * (fan_in ** -0.5)

    def gain(k, shape):
        return 1.0 + 0.02 * jax.random.normal(k, shape, f32)

    dt0 = jnp.exp(jax.random.uniform(ks[8], (L, SSM_HEADS), f32, minval=math.log(1e-3), maxval=math.log(1e-1)))
    return {
        'x': jax.random.normal(ks[0], (BATCH, SEQ, D_MODEL), f32),
        'w_in': nrm(ks[1], (L, D_MODEL, IN_COLS), D_MODEL),
        'w_gate': nrm(ks[2], (L, D_MODEL, 2 * D_MODEL), D_MODEL),
        'w_attn_branch': nrm(ks[3], (L, ATTN_WIDTH, D_MODEL), ATTN_WIDTH),
        'w_ssm_branch': nrm(ks[4], (L, SSM_WIDTH, D_MODEL), SSM_WIDTH),
        'w_out': nrm(ks[5], (L, D_MODEL, D_MODEL), D_MODEL),
        'conv_w': nrm(ks[6], (L, CONV_WIDTH, CONV_CH), CONV_WIDTH),
        'conv_b': 0.02 * jax.random.normal(ks[7], (L, CONV_CH), f32),
        'dt_bias': dt0 + jnp.log(-jnp.expm1(-dt0)),
        'a_log': jnp.log(jax.random.uniform(ks[9], (L, SSM_HEADS), f32, minval=1.0, maxval=16.0)),
        'd_skip': 1.0 + 0.1 * jax.random.normal(ks[10], (L, SSM_HEADS), f32),
        'g_ssm_norm': gain(ks[11], (L, SSM_WIDTH)),
        'g_mix': gain(ks[12], (L, D_MODEL)),
        'g_ffn': gain(ks[13], (L, D_MODEL)),
        'w_ffn_in': nrm(ks[14], (L, D_MODEL, 2 * D_FF), D_MODEL),
        'w_ffn_out': nrm(ks[15], (L, D_FF, D_MODEL), D_FF),
        'g_final': gain(ks[16], (D_MODEL,)),
    }


def reference(x, w_in, w_gate, w_attn_branch, w_ssm_branch, w_out, conv_w, conv_b, dt_bias, a_log, d_skip,
              g_ssm_norm, g_mix, g_ffn, w_ffn_in, w_ffn_out, g_final):
    h = x
    for i in range(DEPTH):
        u = rms_norm(h, g_mix[i])
        h = h + hybrid_mixer(u, w_in[i], w_gate[i], w_attn_branch[i], w_ssm_branch[i], w_out[i], conv_w[i],
                             conv_b[i], dt_bias[i], a_log[i], d_skip[i], g_ssm_norm[i])
        u2 = rms_norm(h, g_ffn[i])
        gate, up = jnp.split(u2 @ w_ffn_in[i], 2, axis=-1)
        h = h + (jax.nn.silu(gate) * up) @ w_ffn_out[i]
    return rms_norm(h, g_final)
```

```python
import functools
import math

import jax
import jax.numpy as jnp
from jax import lax
from jax.experimental import pallas as pl
from jax.experimental.pallas import tpu as pltpu

CHUNK = 64
ROPE_THETA = 10000.0
NORM_EPS = 1e-6
N_HEADS = 16
N_KV_HEADS = 4
HEAD_DIM = 128
TOPK_MAX = 256
N_IDX_HEADS = 16
IDX_DIM = 64
SSM_EXPAND = 2
SSM_HEAD_DIM = 64
SSM_GROUPS = 8
SSM_STATE = 128
CONV_WIDTH = 4

LANES = 128
VMEM_LIMIT = 56 * 1024 * 1024

PROJ_TM = 1024
PROJ_TN = 1024
DSA_TQ = 128
DSA_KT = 256
DSA_ROWS = 64
SSD_TS = 256
SSD_L = 128
MERGE_TM = 512
MERGE_TN = 512
OUT_TM = 512
FFN_TM = 1024
FFN_TN = 512
DOWN_TM = 512
DOWN_TK = 1408

F32 = jnp.float32
BF16 = jnp.bfloat16
INT_MIN = -2147483648
NEG_BIG = -1e30


def _params(sem):
    return pltpu.CompilerParams(dimension_semantics=sem, vmem_limit_bytes=VMEM_LIMIT)


def _silu(x):
    return x * (1.0 / (1.0 + jnp.exp(-x)))


def _rms_body(x_ref, g_ref, o_ref):
    x = x_ref[...]
    y = x * lax.rsqrt(jnp.mean(x * x, axis=-1, keepdims=True) + NORM_EPS)
    o_ref[...] = (y * g_ref[...]).astype(o_ref.dtype)


def _rmsnorm(x2, g, tm=512):
    m, d = x2.shape
    return pl.pallas_call(
        _rms_body,
        grid=(m // tm,),
        in_specs=[pl.BlockSpec((tm, d), lambda i: (i, 0)), pl.BlockSpec((1, d), lambda i: (0, 0))],
        out_specs=pl.BlockSpec((tm, d), lambda i: (i, 0)),
        out_shape=jax.ShapeDtypeStruct((m, d), BF16),
        compiler_params=_params(("parallel",)),
        name="rmsnorm",
    )(x2, g.reshape(1, d))


def _rope_tables(seq, dim):
    pos = jnp.arange(seq, dtype=F32)
    inv = ROPE_THETA ** (-jnp.arange(0, dim, 2, dtype=F32) / dim)
    ang = pos[:, None] * inv[None, :]
    c, s = jnp.cos(ang), jnp.sin(ang)
    reps = LANES // dim
    return (jnp.tile(jnp.concatenate([c, c], axis=-1), (1, reps)),
            jnp.tile(jnp.concatenate([-s, s], axis=-1), (1, reps)))


def _swap_halves(x, dim):
    if dim == LANES:
        return pltpu.roll(x, LANES // 2, axis=1)
    lane = lax.broadcasted_iota(jnp.int32, x.shape, 1)
    first = (lane % dim) < (dim // 2)
    return jnp.where(first, pltpu.roll(x, LANES - dim // 2, axis=1), pltpu.roll(x, dim // 2, axis=1))


def _rope(x, cos_t, sin_t, dim):
    return x * cos_t + _swap_halves(x, dim) * sin_t


def _proj_call(u, w, epilogue, extras, extra_specs, out_shape, out_specs, tm, tn, name):
    bsz, seq, k = u.shape
    n = w.shape[1]

    def body(u_ref, w_ref, *refs):
        acc = jnp.dot(u_ref[0], w_ref[...], preferred_element_type=F32)
        epilogue(acc, *refs)

    return pl.pallas_call(
        body,
        grid=(bsz, seq // tm, n // tn),
        in_specs=[pl.BlockSpec((1, tm, k), lambda b, i, j: (b, i, 0)),
                  pl.BlockSpec((k, tn), lambda b, i, j: (0, j))] + list(extra_specs),
        out_specs=out_specs,
        out_shape=out_shape,
        compiler_params=_params(("parallel", "parallel", "arbitrary")),
        name=name,
    )(u, w, *extras)


def _table_spec(tm):
    return pl.BlockSpec((tm, LANES), lambda b, i, j: (i, 0))


def _rope_epilogue(dim, scale, acc, cos_ref, sin_ref, o_ref):
    c, s = cos_ref[...], sin_ref[...]
    for t in range(acc.shape[1] // LANES):
        sl = slice(t * LANES, (t + 1) * LANES)
        o_ref[0, :, sl] = (_rope(acc[:, sl], c, s, dim) * scale).astype(o_ref.dtype)


def _kv_epilogue(kw, acc, cos_ref, sin_ref, kt_ref, v_ref):
    c, s = cos_ref[...], sin_ref[...]
    for t in range(kw // LANES):
        sl = slice(t * LANES, (t + 1) * LANES)
        kt_ref[0, sl, :] = _rope(acc[:, sl], c, s, HEAD_DIM).T.astype(kt_ref.dtype)
    v_ref[0] = acc[:, kw:].astype(v_ref.dtype)


def _small_epilogue(n_idx, n_dt, acc, cos_ref, sin_ref, bias_ref, kit_ref, w_ref, dt_ref):
    kit = _rope(acc[:, :LANES], cos_ref[...], sin_ref[...], IDX_DIM).T
    kit_ref[0] = kit[:IDX_DIM].astype(kit_ref.dtype)
    w_ref[0] = acc[:, LANES:LANES + n_idx]
    x = acc[:, 2 * LANES:2 * LANES + n_dt] + bias_ref[...]
    dt_ref[0] = jnp.maximum(x, 0.0) + jnp.log1p(jnp.exp(-jnp.abs(x)))


def _plain_epilogue(acc, o_ref):
    o_ref[0] = acc.astype(o_ref.dtype)


def _to_key(x):
    bits = pltpu.bitcast(x, jnp.int32)
    return jnp.where(bits < 0, bits ^ jnp.int32(0x7FFFFFFF), bits)


def _dsa_body(q_ref, qi_ref, w_ref, kit_ref, kt_ref, v_ref, o_ref,
              keys_ref, wb_ref, m_ref, l_ref, acc_ref, *, topk):
    tq = q_ref.shape[1]
    n_idx = w_ref.shape[2]
    n_kv = kt_ref.shape[1] // HEAD_DIM
    grp = q_ref.shape[2] // HEAD_DIM // n_kv
    kt = DSA_KT
    lt = kt // LANES
    q0 = pl.program_id(1) * tq
    n_tiles = (q0 + tq + kt - 1) // kt

    w = w_ref[0]
    for h in range(n_idx):
        wb_ref[h] = jnp.broadcast_to(w[:, h:h + 1], (tq, LANES))

    def idx_tile(j, carry):
        k0 = pl.multiple_of(j * kt, kt)
        kblk = kit_ref[0, :, pl.ds(k0, kt)]
        zero = jnp.zeros_like(kblk)
        rhs = (jnp.concatenate([kblk, zero], axis=0), jnp.concatenate([zero, kblk], axis=0))
        for r in range(tq // DSA_ROWS):
            rows = slice(r * DSA_ROWS, (r + 1) * DSA_ROWS)
            acc = [jnp.zeros((DSA_ROWS, LANES), F32) for _ in range(lt)]
            for p in range(n_idx // 2):
                lhs = qi_ref[0, rows, p * LANES:(p + 1) * LANES]
                for e in range(2):
                    d = jnp.dot(lhs, rhs[e], preferred_element_type=F32)
                    wv = wb_ref[2 * p + e, rows, :]
                    for t in range(lt):
                        acc[t] = acc[t] + jnp.maximum(d[:, t * LANES:(t + 1) * LANES], 0.0) * wv
            row = lax.broadcasted_iota(jnp.int32, (DSA_ROWS, LANES), 0) + (q0 + r * DSA_ROWS)
            limit = (row // CHUNK + 1) * CHUNK
            lane = lax.broadcasted_iota(jnp.int32, (DSA_ROWS, LANES), 1)
            for t in range(lt):
                key = _to_key(acc[t])
                vis = (lane + (k0 + t * LANES)) < limit
                keys_ref[rows, pl.ds(pl.multiple_of(k0 + t * LANES, LANES), LANES)] = jnp.where(vis, key, INT_MIN)
        return carry

    lax.fori_loop(0, n_tiles, idx_tile, 0)

    def count(pred):
        def cnt_tile(j, cnt):
            for t in range(lt):
                k0 = pl.multiple_of(j * kt + t * LANES, LANES)
                cnt = cnt + jnp.where(pred(keys_ref[:, pl.ds(k0, LANES)], k0), 1, 0)
            return cnt

        cnt = lax.fori_loop(0, n_tiles, cnt_tile, jnp.zeros((tq, LANES), jnp.int32))
        return jnp.broadcast_to(jnp.sum(cnt, axis=1, keepdims=True), (tq, LANES))

    def bis_iter(it, carry):
        tau, n_ge = carry
        cand = tau + jnp.left_shift(jnp.int32(1), 31 - it)
        total = count(lambda blk, k0: blk >= cand)
        ok = total >= topk
        return jnp.where(ok, cand, tau), jnp.where(ok, total, n_ge)

    tau, n_ge = lax.fori_loop(0, 32, bis_iter, (jnp.full((tq, LANES), INT_MIN, jnp.int32),
                                                jnp.zeros((tq, LANES), jnp.int32)))
    tau = jnp.maximum(tau, INT_MIN + 1)

    @pl.when(jnp.max(n_ge) > topk)
    def _():
        need = topk - count(lambda blk, k0: blk > tau)
        lane = lax.broadcasted_iota(jnp.int32, (tq, LANES), 1)

        def pos_iter(it, last):
            cand = last + jnp.left_shift(jnp.int32(1), pos_bits - 1 - it)
            below = count(lambda blk, k0: jnp.where(blk == tau, lane + k0, cand) < cand)
            return jnp.where(below < need, cand, last)

        pos_bits = max(1, (kit_ref.shape[2] - 1).bit_length())
        last = lax.fori_loop(0, pos_bits, pos_iter, jnp.zeros((tq, LANES), jnp.int32))

        def demote(j, carry):
            for t in range(lt):
                k0 = pl.multiple_of(j * kt + t * LANES, LANES)
                blk = keys_ref[:, pl.ds(k0, LANES)]
                drop = jnp.where(blk == tau, lane + k0, last) > last
                keys_ref[:, pl.ds(k0, LANES)] = jnp.where(drop, tau - 1, blk)
            return carry

        lax.fori_loop(0, n_tiles, demote, 0)

    m_ref[...] = jnp.full(m_ref.shape, NEG_BIG, F32)
    l_ref[...] = jnp.zeros(l_ref.shape, F32)
    acc_ref[...] = jnp.zeros(acc_ref.shape, F32)

    def att_tile(j, carry):
        k0 = pl.multiple_of(j * kt, kt)
        bias = []
        for t in range(lt):
            blk = keys_ref[:, pl.ds(pl.multiple_of(k0 + t * LANES, LANES), LANES)]
            bias.append(jnp.where(blk >= tau, 0.0, NEG_BIG))
        bias = jnp.concatenate(bias, axis=1)
        bias = jnp.concatenate([bias] * grp, axis=0)
        for g in range(n_kv):
            qs = jnp.concatenate([q_ref[0, :, (g * grp + i) * HEAD_DIM:(g * grp + i + 1) * HEAD_DIM]
                                  for i in range(grp)], axis=0)
            kblk = kt_ref[0, g * HEAD_DIM:(g + 1) * HEAD_DIM, pl.ds(k0, kt)]
            s = jnp.dot(qs, kblk, preferred_element_type=F32) + bias
            m_old = m_ref[g]
            m_new = jnp.maximum(m_old, jnp.max(s, axis=1, keepdims=True))
            alpha = jnp.exp(m_old - m_new)
            p = jnp.exp(s - m_new[:, :1])
            l_ref[g] = alpha * l_ref[g] + jnp.sum(p, axis=1, keepdims=True)
            vblk = v_ref[0, pl.ds(k0, kt), g * HEAD_DIM:(g + 1) * HEAD_DIM]
            acc_ref[g] = alpha * acc_ref[g] + jnp.dot(p.astype(BF16), vblk, preferred_element_type=F32)
            m_ref[g] = m_new
        return carry

    lax.fori_loop(0, n_tiles, att_tile, 0)

    for g in range(n_kv):
        o = acc_ref[g] / l_ref[g]
        for i in range(grp):
            h = g * grp + i
            o_ref[0, :, h * HEAD_DIM:(h + 1) * HEAD_DIM] = o[i * tq:(i + 1) * tq].astype(o_ref.dtype)


def _dsa(q, qi, w_idx, kit, kt, v, topk):
    bsz, seq, aw = q.shape
    tq = DSA_TQ
    n_idx = w_idx.shape[2]
    n_kv = kt.shape[1] // HEAD_DIM
    grp = aw // HEAD_DIM // n_kv
    once = pl.Buffered(1)
    return pl.pallas_call(
        functools.partial(_dsa_body, topk=topk),
        grid=(bsz, seq // tq),
        in_specs=[
            pl.BlockSpec((1, tq, aw), lambda b, i: (b, i, 0)),
            pl.BlockSpec((1, tq, qi.shape[2]), lambda b, i: (b, i, 0)),
            pl.BlockSpec((1, tq, n_idx), lambda b, i: (b, i, 0)),
            pl.BlockSpec((1, IDX_DIM, seq), lambda b, i: (b, 0, 0), pipeline_mode=once),
            pl.BlockSpec((1, kt.shape[1], seq), lambda b, i: (b, 0, 0), pipeline_mode=once),
            pl.BlockSpec((1, seq, v.shape[2]), lambda b, i: (b, 0, 0), pipeline_mode=once),
        ],
        out_specs=pl.BlockSpec((1, tq, aw), lambda b, i: (b, i, 0)),
        out_shape=jax.ShapeDtypeStruct((bsz, seq, aw), BF16),
        scratch_shapes=[
            pltpu.VMEM((tq, seq), jnp.int32),
            pltpu.VMEM((n_idx, tq, LANES), F32),
            pltpu.VMEM((n_kv, grp * tq, LANES), F32),
            pltpu.VMEM((n_kv, grp * tq, LANES), F32),
            pltpu.VMEM((n_kv, grp * tq, HEAD_DIM), F32),
        ],
        compiler_params=_params(("parallel", "arbitrary")),
        name="dsa",
    )(q, qi, w_idx, kit, kt, v)


def _conv_silu(blk_ref, halo_ref, buf_ref, w_ref, b_ref, ts):
    blk = blk_ref[0]
    buf_ref[0:8, :] = halo_ref[...]
    buf_ref[8:8 + ts, :] = blk
    halo_ref[...] = blk[ts - 8:ts, :]
    out = b_ref[...] + w_ref[CONV_WIDTH - 1:CONV_WIDTH, :] * blk
    for k in range(CONV_WIDTH - 1):
        out = out + w_ref[k:k + 1, :] * buf_ref[pl.ds(8 - (CONV_WIDTH - 1) + k, ts), :]
    return _silu(out)


def _ssd_body(z_ref, x_ref, b_ref, c_ref, dt_ref, dtt_ref, alog_row_ref, alog_col_ref, dskip_ref, gn_ref,
              wx_ref, wb_ref, wc_ref, bx_ref, bb_ref, bc_ref, o_ref,
              ht_ref, hx_ref, hb_ref, hc_ref, bufx_ref, bufb_ref, bufc_ref):
    ts = x_ref.shape[1]
    hg = dt_ref.shape[3]
    hd = SSM_HEAD_DIM
    ll = SSD_L

    @pl.when(pl.program_id(2) == 0)
    def _():
        ht_ref[...] = jnp.zeros(ht_ref.shape, F32)
        hx_ref[...] = jnp.zeros(hx_ref.shape, F32)
        hb_ref[...] = jnp.zeros(hb_ref.shape, F32)
        hc_ref[...] = jnp.zeros(hc_ref.shape, F32)

    xs = _conv_silu(x_ref, hx_ref, bufx_ref, wx_ref, bx_ref, ts)
    bm = _conv_silu(b_ref, hb_ref, bufb_ref, wb_ref, bb_ref, ts)
    cm = _conv_silu(c_ref, hc_ref, bufc_ref, wc_ref, bc_ref, ts)

    a_row = -jnp.exp(alog_row_ref[0])
    a_col = -jnp.exp(alog_col_ref[0])
    ti = lax.broadcasted_iota(jnp.int32, (ll, ll), 0)
    si = lax.broadcasted_iota(jnp.int32, (ll, ll), 1)
    causal = si <= ti
    tril = jnp.where(causal, 1.0, 0.0).astype(F32)
    triu = jnp.where(ti <= si, 1.0, 0.0).astype(F32)
    lane = lax.broadcasted_iota(jnp.int32, (1, 2 * hd), 1)
    first = lane < hd

    for ci in range(ts // ll):
        rows = slice(ci * ll, (ci + 1) * ll)
        dt = dt_ref[0, 0, rows, :]
        dtt = dtt_ref[0, 0, :, rows]
        acol = jnp.dot(tril, dt * a_row, preferred_element_type=F32, precision=lax.Precision.HIGHEST)
        arow = jnp.dot(dtt * a_col, triu, preferred_element_type=F32, precision=lax.Precision.HIGHEST)
        alast = arow[:, ll - 1:ll]
        wrow = dtt * jnp.exp(alast - arow)
        elast = jnp.exp(alast)
        xc = xs[rows]
        bc = bm[rows]
        cc = cm[rows]
        bcb = bc.astype(BF16)
        ccb = cc.astype(BF16)
        cb = lax.dot_general(ccb, bcb, (((1,), (1,)), ((), ())), preferred_element_type=F32)
        bt = bc.T
        ys = []
        for p in range(hg // 2):
            xp = xc[:, p * 2 * hd:(p + 1) * 2 * hd].astype(BF16)
            hp = ht_ref[p]
            rhs = jnp.concatenate([xp, hp.astype(BF16)], axis=0)
            y2, s2, e2 = [], [], []
            for e in range(2):
                j = 2 * p + e
                bcol = jnp.broadcast_to(acol[:, j:j + 1], (ll, ll))
                dec = jnp.where(causal, jnp.exp(bcol - arow[j:j + 1, :]), 0.0)
                mm = cb * dec * dtt[j:j + 1, :]
                ecol = jnp.exp(jnp.broadcast_to(acol[:, j:j + 1], (ll, SSM_STATE)))
                lhs = jnp.concatenate([mm, cc * ecol], axis=1).astype(BF16)
                y2.append(jnp.dot(lhs, rhs, preferred_element_type=F32))
                s2.append(jnp.dot((bt * wrow[j:j + 1, :]).astype(BF16), xp, preferred_element_type=F32))
                e2.append(jnp.broadcast_to(elast[j:j + 1, :], (1, 2 * hd)))
            ys.append(jnp.where(first, y2[0], y2[1]))
            ht_ref[p] = hp * jnp.where(first, e2[0], e2[1]) + jnp.where(first, s2[0], s2[1])
        y = jnp.concatenate(ys, axis=1)
        y = (y + dskip_ref[0] * xc) * _silu(z_ref[0, rows, :])
        y = y * lax.rsqrt(jnp.mean(y * y, axis=-1, keepdims=True) + NORM_EPS)
        o_ref[0, rows, :] = (y * gn_ref[0]).astype(o_ref.dtype)


def _ssd(zx, dt, conv_w, conv_b, a_log, d_skip, g_norm):
    bsz, seq, _ = zx.shape
    heads = dt.shape[2]
    g = SSM_GROUPS
    hg = heads // g
    wid = heads * SSM_HEAD_DIM
    gw = wid // g
    n = SSM_STATE
    ts = SSD_TS
    dtg = dt.reshape(bsz, seq, g, hg).transpose(0, 2, 1, 3)
    dtt = dtg.transpose(0, 1, 3, 2)
    zb, xb, bb, cb = 0, wid // gw, 2 * wid // n, (2 * wid + g * n) // n
    cw_x, cw_b, cw_c = conv_w[:, :wid], conv_w[:, wid:wid + g * n], conv_w[:, wid + g * n:]
    cb_x, cb_b, cb_c = (conv_b[None, :wid], conv_b[None, wid:wid + g * n], conv_b[None, wid + g * n:])
    col = lambda off: (lambda b, gi, c: (b, c, off + gi))
    par = lambda b, gi, c: (0, gi)
    return pl.pallas_call(
        _ssd_body,
        grid=(bsz, g, seq // ts),
        in_specs=[
            pl.BlockSpec((1, ts, gw), col(zb)),
            pl.BlockSpec((1, ts, gw), col(xb)),
            pl.BlockSpec((1, ts, n), col(bb)),
            pl.BlockSpec((1, ts, n), col(cb)),
            pl.BlockSpec((1, 1, ts, hg), lambda b, gi, c: (b, gi, c, 0)),
            pl.BlockSpec((1, 1, hg, ts), lambda b, gi, c: (b, gi, 0, c)),
            pl.BlockSpec((1, 1, hg), lambda b, gi, c: (gi, 0, 0)),
            pl.BlockSpec((1, hg, 1), lambda b, gi, c: (gi, 0, 0)),
            pl.BlockSpec((1, 1, gw), lambda b, gi, c: (gi, 0, 0)),
            pl.BlockSpec((1, 1, gw), lambda b, gi, c: (gi, 0, 0)),
            pl.BlockSpec((CONV_WIDTH, gw), par),
            pl.BlockSpec((CONV_WIDTH, n), par),
            pl.BlockSpec((CONV_WIDTH, n), par),
            pl.BlockSpec((1, gw), par),
            pl.BlockSpec((1, n), par),
            pl.BlockSpec((1, n), par),
        ],
        out_specs=pl.BlockSpec((1, ts, gw), lambda b, gi, c: (b, c, gi)),
        out_shape=jax.ShapeDtypeStruct((bsz, seq, wid), BF16),
        scratch_shapes=[
            pltpu.VMEM((hg // 2, n, 2 * SSM_HEAD_DIM), F32),
            pltpu.VMEM((8, gw), F32), pltpu.VMEM((8, n), F32), pltpu.VMEM((8, n), F32),
            pltpu.VMEM((ts + 8, gw), F32), pltpu.VMEM((ts + 8, n), F32), pltpu.VMEM((ts + 8, n), F32),
        ],
        compiler_params=_params(("parallel", "parallel", "arbitrary")),
        name="ssd",
    )(zx, zx, zx, zx, dtg, dtt, a_log.reshape(g, 1, hg), a_log.reshape(g, hg, 1),
      jnp.repeat(d_skip, SSM_HEAD_DIM).reshape(g, 1, gw), g_norm.reshape(g, 1, gw),
      cw_x, cw_b, cw_c, cb_x, cb_b, cb_c)


def _merge_body(u_ref, ya_ref, ys_ref, wg0_ref, wg1_ref, wa_ref, ws_ref, o_ref):
    u = u_ref[...]
    g0 = jax.nn.sigmoid(jnp.dot(u, wg0_ref[...], preferred_element_type=F32))
    g1 = jax.nn.sigmoid(jnp.dot(u, wg1_ref[...], preferred_element_type=F32))
    a = jnp.dot(ya_ref[...], wa_ref[...], preferred_element_type=F32)
    s = jnp.dot(ys_ref[...], ws_ref[...], preferred_element_type=F32)
    o_ref[...] = (g0 * a + g1 * s).astype(o_ref.dtype)


def _merge(u, ya, ys, w_gate, w_a, w_s):
    m, d = u.shape
    tm, tn = MERGE_TM, MERGE_TN
    nj = d // tn
    return pl.pallas_call(
        _merge_body,
        grid=(m // tm, nj),
        in_specs=[
            pl.BlockSpec((tm, d), lambda i, j: (i, 0)),
            pl.BlockSpec((tm, ya.shape[1]), lambda i, j: (i, 0)),
            pl.BlockSpec((tm, ys.shape[1]), lambda i, j: (i, 0)),
            pl.BlockSpec((d, tn), lambda i, j: (0, j)),
            pl.BlockSpec((d, tn), lambda i, j: (0, nj + j)),
            pl.BlockSpec((w_a.shape[0], tn), lambda i, j: (0, j)),
            pl.BlockSpec((w_s.shape[0], tn), lambda i, j: (0, j)),
        ],
        out_specs=pl.BlockSpec((tm, tn), lambda i, j: (i, j)),
        out_shape=jax.ShapeDtypeStruct((m, d), BF16),
        compiler_params=_params(("parallel", "arbitrary")),
        name="merge",
    )(u, ya, ys, w_gate, w_gate, w_a, w_s)


def _out_body(mg_ref, x_ref, w_ref, g_ref, h_ref, u_ref):
    h = x_ref[...] + jnp.dot(mg_ref[...], w_ref[...], preferred_element_type=F32)
    h_ref[...] = h
    y = h * lax.rsqrt(jnp.mean(h * h, axis=-1, keepdims=True) + NORM_EPS)
    u_ref[...] = (y * g_ref[...]).astype(u_ref.dtype)


def _out_proj(merged, x2, w_out, g_ffn):
    m, d = x2.shape
    tm = OUT_TM
    row = pl.BlockSpec((tm, d), lambda i: (i, 0))
    return pl.pallas_call(
        _out_body,
        grid=(m // tm,),
        in_specs=[row, row, pl.BlockSpec((d, d), lambda i: (0, 0)), pl.BlockSpec((1, d), lambda i: (0, 0))],
        out_specs=[row, row],
        out_shape=[jax.ShapeDtypeStruct((m, d), F32), jax.ShapeDtypeStruct((m, d), BF16)],
        compiler_params=_params(("parallel",)),
        name="out_proj",
    )(merged, x2, w_out, g_ffn.reshape(1, d))


def _ffn_up_body(u_ref, wg_ref, wu_ref, o_ref):
    u = u_ref[...]
    gate = jnp.dot(u, wg_ref[...], preferred_element_type=F32)
    up = jnp.dot(u, wu_ref[...], preferred_element_type=F32)
    o_ref[...] = (_silu(gate) * up).astype(o_ref.dtype)


def _ffn_up(u2, w_in):
    m, d = u2.shape
    dff = w_in.shape[1] // 2
    tm, tn = FFN_TM, FFN_TN
    nj = dff // tn
    return pl.pallas_call(
        _ffn_up_body,
        grid=(m // tm, nj),
        in_specs=[pl.BlockSpec((tm, d), lambda i, j: (i, 0)),
                  pl.BlockSpec((d, tn), lambda i, j: (0, j)),
                  pl.BlockSpec((d, tn), lambda i, j: (0, nj + j))],
        out_specs=pl.BlockSpec((tm, tn), lambda i, j: (i, j)),
        out_shape=jax.ShapeDtypeStruct((m, dff), BF16),
        compiler_params=_params(("parallel", "arbitrary")),
        name="ffn_up",
    )(u2, w_in, w_in)


def _ffn_down_body(a_ref, w_ref, h_ref, g_ref, o_ref, acc_ref):
    k = pl.program_id(1)

    @pl.when(k == 0)
    def _():
        acc_ref[...] = h_ref[...]

    acc_ref[...] += jnp.dot(a_ref[...], w_ref[...], preferred_element_type=F32)

    @pl.when(k == pl.num_programs(1) - 1)
    def _():
        h = acc_ref[...]
        y = h * lax.rsqrt(jnp.mean(h * h, axis=-1, keepdims=True) + NORM_EPS)
        o_ref[...] = y * g_ref[...]


def _ffn_down(act, w_out, h1, g_final):
    m, dff = act.shape
    d = w_out.shape[1]
    tm, tk = DOWN_TM, DOWN_TK
    return pl.pallas_call(
        _ffn_down_body,
        grid=(m // tm, dff // tk),
        in_specs=[pl.BlockSpec((tm, tk), lambda i, k: (i, k)),
                  pl.BlockSpec((tk, d), lambda i, k: (k, 0)),
                  pl.BlockSpec((tm, d), lambda i, k: (i, 0)),
                  pl.BlockSpec((1, d), lambda i, k: (0, 0))],
        out_specs=pl.BlockSpec((tm, d), lambda i, k: (i, 0)),
        out_shape=jax.ShapeDtypeStruct((m, d), F32),
        scratch_shapes=[pltpu.VMEM((tm, d), F32)],
        compiler_params=_params(("parallel", "arbitrary")),
        name="ffn_down",
    )(act, w_out, h1, g_final.reshape(1, d))


def _layer(h, w_in, w_gate, w_attn_branch, w_ssm_branch, w_out, conv_w, conv_b, dt_bias, a_log, d_skip,
           g_ssm_norm, g_mix, g_ffn, w_ffn_in, w_ffn_out, g_last):
    bsz, seq, d = h.shape
    m = bsz * seq
    aw = N_HEADS * HEAD_DIM
    kw = N_KV_HEADS * HEAD_DIM
    iw = N_IDX_HEADS * IDX_DIM
    sw = SSM_EXPAND * d
    heads = sw // SSM_HEAD_DIM
    bcw = SSM_GROUPS * SSM_STATE
    cuts = [0]
    for s in (aw, kw, kw, iw, IDX_DIM, N_IDX_HEADS, sw, sw + 2 * bcw, heads):
        cuts.append(cuts[-1] + s)
    wb = w_in.astype(BF16)
    seg = lambda i, j=None: wb[:, cuts[i]:cuts[i + 1 if j is None else j]]
    pad = lambda a, n: jnp.pad(a, ((0, 0), (0, n - a.shape[1])))
    w_small = jnp.concatenate([pad(seg(4), LANES), pad(seg(5), LANES), pad(seg(8), LANES)], axis=1)

    x2 = h.reshape(m, d)
    u = _rmsnorm(x2, g_mix)
    u3 = u.reshape(bsz, seq, d)

    tm = min(PROJ_TM, seq)
    cos_a, sin_a = _rope_tables(seq, HEAD_DIM)
    cos_i, sin_i = _rope_tables(seq, IDX_DIM)
    tab = _table_spec(tm)
    blk = lambda n: pl.BlockSpec((1, tm, n), lambda b, i, j: (b, i, j))

    tn = min(PROJ_TN, aw)
    q = _proj_call(u3, seg(0), functools.partial(_rope_epilogue, HEAD_DIM, HEAD_DIM ** -0.5),
                   (cos_a, sin_a), (tab, tab), jax.ShapeDtypeStruct((bsz, seq, aw), BF16), blk(tn), tm, tn, "proj_q")
    kt, v = _proj_call(u3, seg(1, 3), functools.partial(_kv_epilogue, kw), (cos_a, sin_a), (tab, tab),
                       [jax.ShapeDtypeStruct((bsz, kw, seq), BF16), jax.ShapeDtypeStruct((bsz, seq, kw), BF16)],
                       [pl.BlockSpec((1, kw, tm), lambda b, i, j: (b, 0, i)),
                        pl.BlockSpec((1, tm, kw), lambda b, i, j: (b, i, 0))], tm, 2 * kw, "proj_kv")
    tn = min(PROJ_TN, iw)
    qi = _proj_call(u3, seg(3), functools.partial(_rope_epilogue, IDX_DIM, 1.0),
                    (cos_i, sin_i), (tab, tab), jax.ShapeDtypeStruct((bsz, seq, iw), BF16), blk(tn), tm, tn,
                    "proj_qidx")
    kit, w_idx, dt = _proj_call(
        u3, w_small, functools.partial(_small_epilogue, N_IDX_HEADS, heads),
        (cos_i, sin_i, dt_bias.reshape(1, heads)),
        (tab, tab, pl.BlockSpec((1, heads), lambda b, i, j: (0, 0))),
        [jax.ShapeDtypeStruct((bsz, IDX_DIM, seq), BF16), jax.ShapeDtypeStruct((bsz, seq, N_IDX_HEADS), F32),
         jax.ShapeDtypeStruct((bsz, seq, heads), F32)],
        [pl.BlockSpec((1, IDX_DIM, tm), lambda b, i, j: (b, 0, i)),
         pl.BlockSpec((1, tm, N_IDX_HEADS), lambda b, i, j: (b, i, 0)),
         pl.BlockSpec((1, tm, heads), lambda b, i, j: (b, i, 0))], tm, 3 * LANES, "proj_small")
    nzx = 2 * sw + 2 * bcw
    tn = min(PROJ_TN, nzx)
    zx = _proj_call(u3, seg(6, 8), _plain_epilogue, (), (), jax.ShapeDtypeStruct((bsz, seq, nzx), F32),
                    blk(tn), tm, tn, "proj_zx")

    y_attn = _dsa(q, qi, w_idx, kit, kt, v, min(TOPK_MAX, seq // 4))
    y_ssm = _ssd(zx, dt, conv_w, conv_b, a_log, d_skip, g_ssm_norm)

    merged = _merge(u, y_attn.reshape(m, aw), y_ssm.reshape(m, sw), w_gate.astype(BF16),
                    w_attn_branch.astype(BF16), w_ssm_branch.astype(BF16))
    h1, u2 = _out_proj(merged, x2, w_out.astype(BF16), g_ffn)
    act = _ffn_up(u2, w_ffn_in.astype(BF16))
    return _ffn_down(act, w_ffn_out.astype(BF16), h1, g_last).reshape(bsz, seq, d)


def kernel(x, w_in, w_gate, w_attn_branch, w_ssm_branch, w_out, conv_w, conv_b, dt_bias, a_log, d_skip, g_ssm_norm,
           g_mix, g_ffn, w_ffn_in, w_ffn_out, g_final):
    depth = w_in.shape[0]
    assert depth == 1, "the final norm is fused into the last layer's down projection"
    return _layer(x, w_in[0], w_gate[0], w_attn_branch[0], w_ssm_branch[0], w_out[0], conv_w[0], conv_b[0],
                  dt_bias[0], a_log[0], d_skip[0], g_ssm_norm[0], g_mix[0], g_ffn[0], w_ffn_in[0], w_ffn_out[0],
                  g_final)
```

```python
import functools
import math

import jax
import jax.numpy as jnp
from jax import lax
from jax.experimental import pallas as pl
from jax.experimental.pallas import tpu as pltpu

CHUNK = 64
ROPE_THETA = 10000.0
NORM_EPS = 1e-6
N_HEADS = 16
N_KV_HEADS = 4
HEAD_DIM = 128
TOPK_MAX = 256
N_IDX_HEADS = 16
IDX_DIM = 64
SSM_EXPAND = 2
SSM_HEAD_DIM = 64
SSM_GROUPS = 8
SSM_STATE = 128
CONV_WIDTH = 4

LANES = 128
VMEM_LIMIT = 56 * 1024 * 1024

PROJ_TM = 1024
PROJ_TN = 1024
DSA_IDX_ROWS = 128
DSA_CNT_ROWS = 512
DSA_ATT_ROWS = 256
SSD_TS = 256
SSD_L = 128
MERGE_TM = 512
MERGE_TN = 512
OUT_TM = 512
FFN_TM = 1024
FFN_TN = 512
DOWN_TM = 512
DOWN_TK = 1408

F32 = jnp.float32
BF16 = jnp.bfloat16
INT_MIN = -2147483648
NEG_BIG = -1e30


def _params(sem):
    return pltpu.CompilerParams(dimension_semantics=sem, vmem_limit_bytes=VMEM_LIMIT)


def _silu(x):
    return x * (1.0 / (1.0 + jnp.exp(-x)))


def _rms_body(x_ref, g_ref, o_ref):
    x = x_ref[...]
    y = x * lax.rsqrt(jnp.mean(x * x, axis=-1, keepdims=True) + NORM_EPS)
    o_ref[...] = (y * g_ref[...]).astype(o_ref.dtype)


def _rmsnorm(x2, g, tm=512):
    m, d = x2.shape
    return pl.pallas_call(
        _rms_body,
        grid=(m // tm,),
        in_specs=[pl.BlockSpec((tm, d), lambda i: (i, 0)), pl.BlockSpec((1, d), lambda i: (0, 0))],
        out_specs=pl.BlockSpec((tm, d), lambda i: (i, 0)),
        out_shape=jax.ShapeDtypeStruct((m, d), BF16),
        compiler_params=_params(("parallel",)),
        name="rmsnorm",
    )(x2, g.reshape(1, d))


def _rope_tables(seq, dim):
    pos = jnp.arange(seq, dtype=F32)
    inv = ROPE_THETA ** (-jnp.arange(0, dim, 2, dtype=F32) / dim)
    ang = pos[:, None] * inv[None, :]
    c, s = jnp.cos(ang), jnp.sin(ang)
    reps = LANES // dim
    return (jnp.tile(jnp.concatenate([c, c], axis=-1), (1, reps)),
            jnp.tile(jnp.concatenate([-s, s], axis=-1), (1, reps)))


def _swap_halves(x, dim):
    if dim == LANES:
        return pltpu.roll(x, LANES // 2, axis=1)
    lane = lax.broadcasted_iota(jnp.int32, x.shape, 1)
    first = (lane % dim) < (dim // 2)
    return jnp.where(first, pltpu.roll(x, LANES - dim // 2, axis=1), pltpu.roll(x, dim // 2, axis=1))


def _rope(x, cos_t, sin_t, dim):
    return x * cos_t + _swap_halves(x, dim) * sin_t


def _proj_call(u, w, epilogue, extras, extra_specs, out_shape, out_specs, tm, tn, name):
    bsz, seq, k = u.shape
    n = w.shape[1]

    def body(u_ref, w_ref, *refs):
        acc = jnp.dot(u_ref[0], w_ref[...], preferred_element_type=F32)
        epilogue(acc, *refs)

    return pl.pallas_call(
        body,
        grid=(bsz, seq // tm, n // tn),
        in_specs=[pl.BlockSpec((1, tm, k), lambda b, i, j: (b, i, 0)),
                  pl.BlockSpec((k, tn), lambda b, i, j: (0, j))] + list(extra_specs),
        out_specs=out_specs,
        out_shape=out_shape,
        compiler_params=_params(("parallel", "parallel", "arbitrary")),
        name=name,
    )(u, w, *extras)


def _table_spec(tm):
    return pl.BlockSpec((tm, LANES), lambda b, i, j: (i, 0))


def _rope_t_epilogue(dim, scale, acc, cos_ref, sin_ref, o_ref):
    c, s = cos_ref[...], sin_ref[...]
    for t in range(acc.shape[1] // LANES):
        sl = slice(t * LANES, (t + 1) * LANES)
        o_ref[0, sl, :] = (_rope(acc[:, sl], c, s, dim) * scale).T.astype(o_ref.dtype)


def _kv_epilogue(kw, acc, cos_ref, sin_ref, k_ref, vt_ref):
    c, s = cos_ref[...], sin_ref[...]
    for t in range(kw // LANES):
        sl = slice(t * LANES, (t + 1) * LANES)
        k_ref[0, :, sl] = _rope(acc[:, sl], c, s, HEAD_DIM).astype(k_ref.dtype)
        vt_ref[0, sl, :] = acc[:, kw + t * LANES:kw + (t + 1) * LANES].T.astype(vt_ref.dtype)


def _small_epilogue(n_idx, n_dt, acc, cos_ref, sin_ref, bias_ref, k2_ref, wt_ref, dt_ref):
    k2_ref[0] = _rope(acc[:, :LANES], cos_ref[...], sin_ref[...], IDX_DIM).astype(k2_ref.dtype)
    wt_ref[0] = acc[:, LANES:2 * LANES].T[:n_idx]
    x = acc[:, 2 * LANES:2 * LANES + n_dt] + bias_ref[...]
    dt_ref[0] = jnp.maximum(x, 0.0) + jnp.log1p(jnp.exp(-jnp.abs(x)))


def _plain_epilogue(acc, o_ref):
    o_ref[0] = acc.astype(o_ref.dtype)


def _to_key(x):
    bits = pltpu.bitcast(x, jnp.int32)
    return jnp.where(bits < 0, bits ^ jnp.int32(0x7FFFFFFF), bits)


def _dsa_body(qt_ref, qit_ref, wt_ref, k2_ref, k_ref, vt_ref, o_ref,
              keys_ref, rhs_ref, m_ref, l_ref, acc_ref, s_ref, smax_ref, *, topk):
    tq = qt_ref.shape[2]
    n_idx = wt_ref.shape[1]
    seq = k_ref.shape[1]
    n_kv = k_ref.shape[2] // HEAD_DIM
    grp = qt_ref.shape[1] // HEAD_DIM // n_kv
    ir, cr, ar = DSA_IDX_ROWS, DSA_CNT_ROWS, DSA_ATT_ROWS
    q0 = pl.program_id(1) * tq
    n_ct = (q0 + tq + cr - 1) // cr
    n_at = (q0 + tq + ar - 1) // ar

    zero = jnp.zeros((IDX_DIM, tq), rhs_ref.dtype)
    for p in range(n_idx // 2):
        top = qit_ref[0, 2 * p * IDX_DIM:(2 * p + 1) * IDX_DIM, :]
        bot = qit_ref[0, (2 * p + 1) * IDX_DIM:(2 * p + 2) * IDX_DIM, :]
        rhs_ref[p] = jnp.concatenate([jnp.concatenate([top, zero], axis=1),
                                      jnp.concatenate([zero, bot], axis=1)], axis=0)

    qpos = q0 + lax.broadcasted_iota(jnp.int32, (ir, tq), 1)
    limit = (qpos // CHUNK + 1) * CHUNK
    krow = lax.broadcasted_iota(jnp.int32, (ir, tq), 0)

    def idx_tile(j, carry):
        for sub in range(cr // ir):
            k0 = pl.multiple_of(j * cr + sub * ir, ir)
            kblk = k2_ref[0, pl.ds(k0, ir), :]
            acc = jnp.zeros((ir, tq), F32)
            for p in range(n_idx // 2):
                d = jnp.dot(kblk, rhs_ref[p], preferred_element_type=F32)
                acc = acc + jnp.maximum(d[:, :tq], 0.0) * wt_ref[0, 2 * p:2 * p + 1, :]
                acc = acc + jnp.maximum(d[:, tq:], 0.0) * wt_ref[0, 2 * p + 1:2 * p + 2, :]
            keys_ref[pl.ds(k0, ir), :] = jnp.where(krow + k0 < limit, _to_key(acc), INT_MIN)
        return carry

    lax.fori_loop(0, n_ct, idx_tile, 0)

    def count(pred):
        def cnt_tile(j, cnt):
            k0 = pl.multiple_of(j * cr, cr)
            hit = jnp.where(pred(keys_ref[pl.ds(k0, cr), :], k0), 1, 0)
            return cnt + jnp.sum(hit.reshape(cr // 8, 8, tq), axis=0)

        cnt = lax.fori_loop(0, n_ct, cnt_tile, jnp.zeros((8, tq), jnp.int32))
        return jnp.sum(cnt, axis=0, keepdims=True)

    def bis_iter(it, carry):
        tau, n_ge = carry
        cand = tau + jnp.left_shift(jnp.int32(1), 31 - it)
        total = count(lambda blk, k0: blk >= cand)
        ok = total >= topk
        return jnp.where(ok, cand, tau), jnp.where(ok, total, n_ge)

    tau, n_ge = lax.fori_loop(0, 32, bis_iter, (jnp.full((1, tq), INT_MIN, jnp.int32),
                                                jnp.zeros((1, tq), jnp.int32)))
    tau = jnp.maximum(tau, INT_MIN + 1)

    @pl.when(jnp.max(n_ge) > topk)
    def _():
        need = topk - count(lambda blk, k0: blk > tau)
        pos = lax.broadcasted_iota(jnp.int32, (cr, tq), 0)
        pos_bits = max(1, (seq - 1).bit_length())

        def pos_iter(it, last):
            cand = last + jnp.left_shift(jnp.int32(1), pos_bits - 1 - it)
            below = count(lambda blk, k0: jnp.where(blk == tau, pos + k0, cand) < cand)
            return jnp.where(below < need, cand, last)

        last = lax.fori_loop(0, pos_bits, pos_iter, jnp.zeros((1, tq), jnp.int32))

        def demote(j, carry):
            k0 = pl.multiple_of(j * cr, cr)
            blk = keys_ref[pl.ds(k0, cr), :]
            drop = jnp.where(blk == tau, pos + k0, last) > last
            keys_ref[pl.ds(k0, cr), :] = jnp.where(drop, tau - 1, blk)
            return carry

        lax.fori_loop(0, n_ct, demote, 0)

    m_ref[...] = jnp.full(m_ref.shape, NEG_BIG, F32)
    l_ref[...] = jnp.zeros(l_ref.shape, F32)
    acc_ref[...] = jnp.zeros(acc_ref.shape, F32)

    def logits_stage(j, slot):
        k0 = pl.multiple_of(j * ar, ar)
        bias = jnp.where(keys_ref[pl.ds(k0, ar), :] >= tau, 0.0, NEG_BIG)
        bias = jnp.concatenate([bias] * grp, axis=1)
        for g in range(n_kv):
            qtg = jnp.concatenate([qt_ref[0, (g * grp + i) * HEAD_DIM:(g * grp + i + 1) * HEAD_DIM, :]
                                   for i in range(grp)], axis=1)
            kg = k_ref[0, pl.ds(k0, ar), g * HEAD_DIM:(g + 1) * HEAD_DIM]
            s = jnp.dot(kg, qtg, preferred_element_type=F32) + bias
            s_ref[slot, g] = s
            smax_ref[slot, g] = jnp.max(s, axis=0, keepdims=True)

    def softmax_stage(j, slot):
        k0 = pl.multiple_of(j * ar, ar)
        for g in range(n_kv):
            m_old = m_ref[g]
            m_new = jnp.maximum(m_old, smax_ref[slot, g])
            alpha = jnp.exp2(m_old - m_new)
            p = jnp.exp2(s_ref[slot, g] - m_new)
            l_ref[g] = alpha * l_ref[g] + jnp.sum(p, axis=0, keepdims=True)
            vtg = vt_ref[0, g * HEAD_DIM:(g + 1) * HEAD_DIM, pl.ds(k0, ar)]
            acc_ref[g] = alpha * acc_ref[g] + jnp.dot(vtg, p.astype(BF16), preferred_element_type=F32)
            m_ref[g] = m_new

    logits_stage(0, 0)

    def att_tile(j, carry):
        slot = j % 2
        softmax_stage(j, slot)
        logits_stage(jnp.minimum(j + 1, n_at - 1), 1 - slot)
        return carry

    lax.fori_loop(0, n_at, att_tile, 0)

    for g in range(n_kv):
        o = acc_ref[g] / l_ref[g]
        for i in range(grp):
            h = g * grp + i
            o_ref[0, :, h * HEAD_DIM:(h + 1) * HEAD_DIM] = o[:, i * tq:(i + 1) * tq].T.astype(o_ref.dtype)


def _dsa(qt, qit, wt, k2, k, vt, topk):
    bsz, aw, seq = qt.shape
    tq = LANES
    n_idx = wt.shape[1]
    kw = k.shape[2]
    n_kv = kw // HEAD_DIM
    grp = aw // kw
    once = pl.Buffered(1)
    return pl.pallas_call(
        functools.partial(_dsa_body, topk=topk),
        grid=(bsz, seq // tq),
        in_specs=[
            pl.BlockSpec((1, aw, tq), lambda b, i: (b, 0, i)),
            pl.BlockSpec((1, qit.shape[1], tq), lambda b, i: (b, 0, i)),
            pl.BlockSpec((1, n_idx, tq), lambda b, i: (b, 0, i)),
            pl.BlockSpec((1, seq, 2 * IDX_DIM), lambda b, i: (b, 0, 0), pipeline_mode=once),
            pl.BlockSpec((1, seq, kw), lambda b, i: (b, 0, 0), pipeline_mode=once),
            pl.BlockSpec((1, kw, seq), lambda b, i: (b, 0, 0), pipeline_mode=once),
        ],
        out_specs=pl.BlockSpec((1, tq, aw), lambda b, i: (b, i, 0)),
        out_shape=jax.ShapeDtypeStruct((bsz, seq, aw), BF16),
        scratch_shapes=[
            pltpu.VMEM((seq, tq), jnp.int32),
            pltpu.VMEM((n_idx // 2, 2 * IDX_DIM, 2 * tq), BF16),
            pltpu.VMEM((n_kv, 1, grp * tq), F32),
            pltpu.VMEM((n_kv, 1, grp * tq), F32),
            pltpu.VMEM((n_kv, HEAD_DIM, grp * tq), F32),
            pltpu.VMEM((2, n_kv, DSA_ATT_ROWS, grp * tq), F32),
            pltpu.VMEM((2, n_kv, 1, grp * tq), F32),
        ],
        compiler_params=_params(("parallel", "arbitrary")),
        name="dsa",
    )(qt, qit, wt, k2, k, vt)


def _conv_silu(blk_ref, halo_ref, buf_ref, w_ref, b_ref, ts):
    blk = blk_ref[0]
    buf_ref[0:8, :] = halo_ref[...]
    buf_ref[8:8 + ts, :] = blk
    halo_ref[...] = blk[ts - 8:ts, :]
    out = b_ref[...] + w_ref[CONV_WIDTH - 1:CONV_WIDTH, :] * blk
    for k in range(CONV_WIDTH - 1):
        out = out + w_ref[k:k + 1, :] * buf_ref[pl.ds(8 - (CONV_WIDTH - 1) + k, ts), :]
    return _silu(out)


def _ssd_body(z_ref, x_ref, b_ref, c_ref, dt_ref, dtt_ref, alog_row_ref, alog_col_ref, dskip_ref, gn_ref,
              wx_ref, wb_ref, wc_ref, bx_ref, bb_ref, bc_ref, o_ref,
              ht_ref, hx_ref, hb_ref, hc_ref, bufx_ref, bufb_ref, bufc_ref):
    ts = x_ref.shape[1]
    hg = dt_ref.shape[3]
    hd = SSM_HEAD_DIM
    ll = SSD_L

    @pl.when(pl.program_id(2) == 0)
    def _():
        ht_ref[...] = jnp.zeros(ht_ref.shape, F32)
        hx_ref[...] = jnp.zeros(hx_ref.shape, F32)
        hb_ref[...] = jnp.zeros(hb_ref.shape, F32)
        hc_ref[...] = jnp.zeros(hc_ref.shape, F32)

    xs = _conv_silu(x_ref, hx_ref, bufx_ref, wx_ref, bx_ref, ts)
    bm = _conv_silu(b_ref, hb_ref, bufb_ref, wb_ref, bb_ref, ts)
    cm = _conv_silu(c_ref, hc_ref, bufc_ref, wc_ref, bc_ref, ts)

    a_row = -jnp.exp(alog_row_ref[0])
    a_col = -jnp.exp(alog_col_ref[0])
    ti = lax.broadcasted_iota(jnp.int32, (ll, ll), 0)
    si = lax.broadcasted_iota(jnp.int32, (ll, ll), 1)
    causal = si <= ti
    tril = jnp.where(causal, 1.0, 0.0).astype(F32)
    triu = jnp.where(ti <= si, 1.0, 0.0).astype(F32)
    lane = lax.broadcasted_iota(jnp.int32, (1, 2 * hd), 1)
    first = lane < hd

    for ci in range(ts // ll):
        rows = slice(ci * ll, (ci + 1) * ll)
        dt = dt_ref[0, 0, rows, :]
        dtt = dtt_ref[0, 0, :, rows]
        acol = jnp.dot(tril, dt * a_row, preferred_element_type=F32, precision=lax.Precision.HIGHEST)
        arow = jnp.dot(dtt * a_col, triu, preferred_element_type=F32, precision=lax.Precision.HIGHEST)
        alast = arow[:, ll - 1:ll]
        wrow = dtt * jnp.exp(alast - arow)
        elast = jnp.exp(alast)
        xc = xs[rows]
        bc = bm[rows]
        cc = cm[rows]
        bcb = bc.astype(BF16)
        ccb = cc.astype(BF16)
        cb = lax.dot_general(ccb, bcb, (((1,), (1,)), ((), ())), preferred_element_type=F32)
        bt = bc.T
        ys = []
        for p in range(hg // 2):
            xp = xc[:, p * 2 * hd:(p + 1) * 2 * hd].astype(BF16)
            hp = ht_ref[p]
            rhs = jnp.concatenate([xp, hp.astype(BF16)], axis=0)
            y2, s2, e2 = [], [], []
            for e in range(2):
                j = 2 * p + e
                bcol = jnp.broadcast_to(acol[:, j:j + 1], (ll, ll))
                dec = jnp.where(causal, jnp.exp(bcol - arow[j:j + 1, :]), 0.0)
                mm = cb * dec * dtt[j:j + 1, :]
                ecol = jnp.exp(jnp.broadcast_to(acol[:, j:j + 1], (ll, SSM_STATE)))
                lhs = jnp.concatenate([mm, cc * ecol], axis=1).astype(BF16)
                y2.append(jnp.dot(lhs, rhs, preferred_element_type=F32))
                s2.append(jnp.dot((bt * wrow[j:j + 1, :]).astype(BF16), xp, preferred_element_type=F32))
                e2.append(jnp.broadcast_to(elast[j:j + 1, :], (1, 2 * hd)))
            ys.append(jnp.where(first, y2[0], y2[1]))
            ht_ref[p] = hp * jnp.where(first, e2[0], e2[1]) + jnp.where(first, s2[0], s2[1])
        y = jnp.concatenate(ys, axis=1)
        y = (y + dskip_ref[0] * xc) * _silu(z_ref[0, rows, :])
        y = y * lax.rsqrt(jnp.mean(y * y, axis=-1, keepdims=True) + NORM_EPS)
        o_ref[0, rows, :] = (y * gn_ref[0]).astype(o_ref.dtype)


def _ssd(zx, dt, conv_w, conv_b, a_log, d_skip, g_norm):
    bsz, seq, _ = zx.shape
    heads = dt.shape[2]
    g = SSM_GROUPS
    hg = heads // g
    wid = heads * SSM_HEAD_DIM
    gw = wid // g
    n = SSM_STATE
    ts = SSD_TS
    dtg = dt.reshape(bsz, seq, g, hg).transpose(0, 2, 1, 3)
    dtt = dtg.transpose(0, 1, 3, 2)
    zb, xb, bb, cb = 0, wid // gw, 2 * wid // n, (2 * wid + g * n) // n
    cw_x, cw_b, cw_c = conv_w[:, :wid], conv_w[:, wid:wid + g * n], conv_w[:, wid + g * n:]
    cb_x, cb_b, cb_c = (conv_b[None, :wid], conv_b[None, wid:wid + g * n], conv_b[None, wid + g * n:])
    col = lambda off: (lambda b, gi, c: (b, c, off + gi))
    par = lambda b, gi, c: (0, gi)
    return pl.pallas_call(
        _ssd_body,
        grid=(bsz, g, seq // ts),
        in_specs=[
            pl.BlockSpec((1, ts, gw), col(zb)),
            pl.BlockSpec((1, ts, gw), col(xb)),
            pl.BlockSpec((1, ts, n), col(bb)),
            pl.BlockSpec((1, ts, n), col(cb)),
            pl.BlockSpec((1, 1, ts, hg), lambda b, gi, c: (b, gi, c, 0)),
            pl.BlockSpec((1, 1, hg, ts), lambda b, gi, c: (b, gi, 0, c)),
            pl.BlockSpec((1, 1, hg), lambda b, gi, c: (gi, 0, 0)),
            pl.BlockSpec((1, hg, 1), lambda b, gi, c: (gi, 0, 0)),
            pl.BlockSpec((1, 1, gw), lambda b, gi, c: (gi, 0, 0)),
            pl.BlockSpec((1, 1, gw), lambda b, gi, c: (gi, 0, 0)),
            pl.BlockSpec((CONV_WIDTH, gw), par),
            pl.BlockSpec((CONV_WIDTH, n), par),
            pl.BlockSpec((CONV_WIDTH, n), par),
            pl.BlockSpec((1, gw), par),
            pl.BlockSpec((1, n), par),
            pl.BlockSpec((1, n), par),
        ],
        out_specs=pl.BlockSpec((1, ts, gw), lambda b, gi, c: (b, c, gi)),
        out_shape=jax.ShapeDtypeStruct((bsz, seq, wid), BF16),
        scratch_shapes=[
            pltpu.VMEM((hg // 2, n, 2 * SSM_HEAD_DIM), F32),
            pltpu.VMEM((8, gw), F32), pltpu.VMEM((8, n), F32), pltpu.VMEM((8, n), F32),
            pltpu.VMEM((ts + 8, gw), F32), pltpu.VMEM((ts + 8, n), F32), pltpu.VMEM((ts + 8, n), F32),
        ],
        compiler_params=_params(("parallel", "parallel", "arbitrary")),
        name="ssd",
    )(zx, zx, zx, zx, dtg, dtt, a_log.reshape(g, 1, hg), a_log.reshape(g, hg, 1),
      jnp.repeat(d_skip, SSM_HEAD_DIM).reshape(g, 1, gw), g_norm.reshape(g, 1, gw),
      cw_x, cw_b, cw_c, cb_x, cb_b, cb_c)


def _merge_body(u_ref, ya_ref, ys_ref, wg0_ref, wg1_ref, wa_ref, ws_ref, o_ref):
    u = u_ref[...]
    g0 = jax.nn.sigmoid(jnp.dot(u, wg0_ref[...], preferred_element_type=F32))
    g1 = jax.nn.sigmoid(jnp.dot(u, wg1_ref[...], preferred_element_type=F32))
    a = jnp.dot(ya_ref[...], wa_ref[...], preferred_element_type=F32)
    s = jnp.dot(ys_ref[...], ws_ref[...], preferred_element_type=F32)
    o_ref[...] = (g0 * a + g1 * s).astype(o_ref.dtype)


def _merge(u, ya, ys, w_gate, w_a, w_s):
    m, d = u.shape
    tm, tn = MERGE_TM, MERGE_TN
    nj = d // tn
    return pl.pallas_call(
        _merge_body,
        grid=(m // tm, nj),
        in_specs=[
            pl.BlockSpec((tm, d), lambda i, j: (i, 0)),
            pl.BlockSpec((tm, ya.shape[1]), lambda i, j: (i, 0)),
            pl.BlockSpec((tm, ys.shape[1]), lambda i, j: (i, 0)),
            pl.BlockSpec((d, tn), lambda i, j: (0, j)),
            pl.BlockSpec((d, tn), lambda i, j: (0, nj + j)),
            pl.BlockSpec((w_a.shape[0], tn), lambda i, j: (0, j)),
            pl.BlockSpec((w_s.shape[0], tn), lambda i, j: (0, j)),
        ],
        out_specs=pl.BlockSpec((tm, tn), lambda i, j: (i, j)),
        out_shape=jax.ShapeDtypeStruct((m, d), BF16),
        compiler_params=_params(("parallel", "arbitrary")),
        name="merge",
    )(u, ya, ys, w_gate, w_gate, w_a, w_s)


def _out_body(mg_ref, x_ref, w_ref, g_ref, h_ref, u_ref):
    h = x_ref[...] + jnp.dot(mg_ref[...], w_ref[...], preferred_element_type=F32)
    h_ref[...] = h
    y = h * lax.rsqrt(jnp.mean(h * h, axis=-1, keepdims=True) + NORM_EPS)
    u_ref[...] = (y * g_ref[...]).astype(u_ref.dtype)


def _out_proj(merged, x2, w_out, g_ffn):
    m, d = x2.shape
    tm = OUT_TM
    row = pl.BlockSpec((tm, d), lambda i: (i, 0))
    return pl.pallas_call(
        _out_body,
        grid=(m // tm,),
        in_specs=[row, row, pl.BlockSpec((d, d), lambda i: (0, 0)), pl.BlockSpec((1, d), lambda i: (0, 0))],
        out_specs=[row, row],
        out_shape=[jax.ShapeDtypeStruct((m, d), F32), jax.ShapeDtypeStruct((m, d), BF16)],
        compiler_params=_params(("parallel",)),
        name="out_proj",
    )(merged, x2, w_out, g_ffn.reshape(1, d))


def _ffn_up_body(u_ref, wg_ref, wu_ref, o_ref):
    u = u_ref[...]
    gate = jnp.dot(u, wg_ref[...], preferred_element_type=F32)
    up = jnp.dot(u, wu_ref[...], preferred_element_type=F32)
    o_ref[...] = (_silu(gate) * up).astype(o_ref.dtype)


def _ffn_up(u2, w_in):
    m, d = u2.shape
    dff = w_in.shape[1] // 2
    tm, tn = FFN_TM, FFN_TN
    nj = dff // tn
    return pl.pallas_call(
        _ffn_up_body,
        grid=(m // tm, nj),
        in_specs=[pl.BlockSpec((tm, d), lambda i, j: (i, 0)),
                  pl.BlockSpec((d, tn), lambda i, j: (0, j)),
                  pl.BlockSpec((d, tn), lambda i, j: (0, nj + j))],
        out_specs=pl.BlockSpec((tm, tn), lambda i, j: (i, j)),
        out_shape=jax.ShapeDtypeStruct((m, dff), BF16),
        compiler_params=_params(("parallel", "arbitrary")),
        name="ffn_up",
    )(u2, w_in, w_in)


def _ffn_down_body(a_ref, w_ref, h_ref, g_ref, o_ref, acc_ref):
    k = pl.program_id(1)

    @pl.when(k == 0)
    def _():
        acc_ref[...] = h_ref[...]

    acc_ref[...] += jnp.dot(a_ref[...], w_ref[...], preferred_element_type=F32)

    @pl.when(k == pl.num_programs(1) - 1)
    def _():
        h = acc_ref[...]
        y = h * lax.rsqrt(jnp.mean(h * h, axis=-1, keepdims=True) + NORM_EPS)
        o_ref[...] = y * g_ref[...]


def _ffn_down(act, w_out, h1, g_final):
    m, dff = act.shape
    d = w_out.shape[1]
    tm, tk = DOWN_TM, DOWN_TK
    return pl.pallas_call(
        _ffn_down_body,
        grid=(m // tm, dff // tk),
        in_specs=[pl.BlockSpec((tm, tk), lambda i, k: (i, k)),
                  pl.BlockSpec((tk, d), lambda i, k: (k, 0)),
                  pl.BlockSpec((tm, d), lambda i, k: (i, 0)),
                  pl.BlockSpec((1, d), lambda i, k: (0, 0))],
        out_specs=pl.BlockSpec((tm, d), lambda i, k: (i, 0)),
        out_shape=jax.ShapeDtypeStruct((m, d), F32),
        scratch_shapes=[pltpu.VMEM((tm, d), F32)],
        compiler_params=_params(("parallel", "arbitrary")),
        name="ffn_down",
    )(act, w_out, h1, g_final.reshape(1, d))


def _layer(h, w_in, w_gate, w_attn_branch, w_ssm_branch, w_out, conv_w, conv_b, dt_bias, a_log, d_skip,
           g_ssm_norm, g_mix, g_ffn, w_ffn_in, w_ffn_out, g_last):
    bsz, seq, d = h.shape
    m = bsz * seq
    aw = N_HEADS * HEAD_DIM
    kw = N_KV_HEADS * HEAD_DIM
    iw = N_IDX_HEADS * IDX_DIM
    sw = SSM_EXPAND * d
    heads = sw // SSM_HEAD_DIM
    bcw = SSM_GROUPS * SSM_STATE
    cuts = [0]
    for s in (aw, kw, kw, iw, IDX_DIM, N_IDX_HEADS, sw, sw + 2 * bcw, heads):
        cuts.append(cuts[-1] + s)
    wb = w_in.astype(BF16)
    seg = lambda i, j=None: wb[:, cuts[i]:cuts[i + 1 if j is None else j]]
    pad = lambda a, n: jnp.pad(a, ((0, 0), (0, n - a.shape[1])))
    w_small = jnp.concatenate([seg(4), seg(4), pad(seg(5), LANES), pad(seg(8), LANES)], axis=1)

    x2 = h.reshape(m, d)
    u = _rmsnorm(x2, g_mix)
    u3 = u.reshape(bsz, seq, d)

    tm = min(PROJ_TM, seq)
    cos_a, sin_a = _rope_tables(seq, HEAD_DIM)
    cos_i, sin_i = _rope_tables(seq, IDX_DIM)
    tab = _table_spec(tm)
    blk = lambda n: pl.BlockSpec((1, tm, n), lambda b, i, j: (b, i, j))
    blk_t = lambda n: pl.BlockSpec((1, n, tm), lambda b, i, j: (b, j, i))

    tn = min(PROJ_TN, aw)
    qt = _proj_call(u3, seg(0), functools.partial(_rope_t_epilogue, HEAD_DIM, HEAD_DIM ** -0.5 * math.log2(math.e)),
                    (cos_a, sin_a), (tab, tab), jax.ShapeDtypeStruct((bsz, aw, seq), BF16), blk_t(tn), tm, tn,
                    "proj_q")
    k, vt = _proj_call(u3, seg(1, 3), functools.partial(_kv_epilogue, kw), (cos_a, sin_a), (tab, tab),
                       [jax.ShapeDtypeStruct((bsz, seq, kw), BF16), jax.ShapeDtypeStruct((bsz, kw, seq), BF16)],
                       [pl.BlockSpec((1, tm, kw), lambda b, i, j: (b, i, 0)),
                        pl.BlockSpec((1, kw, tm), lambda b, i, j: (b, 0, i))], tm, 2 * kw, "proj_kv")
    tn = min(PROJ_TN, iw)
    qit = _proj_call(u3, seg(3), functools.partial(_rope_t_epilogue, IDX_DIM, 1.0),
                     (cos_i, sin_i), (tab, tab), jax.ShapeDtypeStruct((bsz, iw, seq), BF16), blk_t(tn), tm, tn,
                     "proj_qidx")
    k2, wt, dt = _proj_call(
        u3, w_small, functools.partial(_small_epilogue, N_IDX_HEADS, heads),
        (cos_i, sin_i, dt_bias.reshape(1, heads)),
        (tab, tab, pl.BlockSpec((1, heads), lambda b, i, j: (0, 0))),
        [jax.ShapeDtypeStruct((bsz, seq, 2 * IDX_DIM), BF16), jax.ShapeDtypeStruct((bsz, N_IDX_HEADS, seq), F32),
         jax.ShapeDtypeStruct((bsz, seq, heads), F32)],
        [pl.BlockSpec((1, tm, 2 * IDX_DIM), lambda b, i, j: (b, i, 0)),
         pl.BlockSpec((1, N_IDX_HEADS, tm), lambda b, i, j: (b, 0, i)),
         pl.BlockSpec((1, tm, heads), lambda b, i, j: (b, i, 0))], tm, 3 * LANES, "proj_small")
    nzx = 2 * sw + 2 * bcw
    tn = min(PROJ_TN, nzx)
    zx = _proj_call(u3, seg(6, 8), _plain_epilogue, (), (), jax.ShapeDtypeStruct((bsz, seq, nzx), F32),
                    blk(tn), tm, tn, "proj_zx")

    y_attn = _dsa(qt, qit, wt, k2, k, vt, min(TOPK_MAX, seq // 4))
    y_ssm = _ssd(zx, dt, conv_w, conv_b, a_log, d_skip, g_ssm_norm)

    merged = _merge(u, y_attn.reshape(m, aw), y_ssm.reshape(m, sw), w_gate.astype(BF16),
                    w_attn_branch.astype(BF16), w_ssm_branch.astype(BF16))
    h1, u2 = _out_proj(merged, x2, w_out.astype(BF16), g_ffn)
    act = _ffn_up(u2, w_ffn_in.astype(BF16))
    return _ffn_down(act, w_ffn_out.astype(BF16), h1, g_last).reshape(bsz, seq, d)


def kernel(x, w_in, w_gate, w_attn_branch, w_ssm_branch, w_out, conv_w, conv_b, dt_bias, a_log, d_skip, g_ssm_norm,
           g_mix, g_ffn, w_ffn_in, w_ffn_out, g_final):
    depth = w_in.shape[0]
    assert depth == 1, "the final norm is fused into the last layer's down projection"
    return _layer(x, w_in[0], w_gate[0], w_attn_branch[0], w_ssm_branch[0], w_out[0], conv_w[0], conv_b[0],
                  dt_bias[0], a_log[0], d_skip[0], g_ssm_norm[0], g_mix[0], g_ffn[0], w_ffn_in[0], w_ffn_out[0],
                  g_final)
```

```python
import functools
import math

import jax
import jax.numpy as jnp
from jax import lax
from jax.experimental import pallas as pl
from jax.experimental.pallas import tpu as pltpu

CHUNK = 64
ROPE_THETA = 10000.0
NORM_EPS = 1e-6
N_HEADS = 16
N_KV_HEADS = 4
HEAD_DIM = 128
TOPK_MAX = 256
N_IDX_HEADS = 16
IDX_DIM = 64
SSM_EXPAND = 2
SSM_HEAD_DIM = 64
SSM_GROUPS = 8
SSM_STATE = 128
CONV_WIDTH = 4

LANES = 128
VMEM_LIMIT = 60 * 1024 * 1024

PROJ_TM = 1024
PROJ_TN = 1024
DSA_IDX_ROWS = 128
DSA_CNT_ROWS = 512
DSA_ATT_ROWS = 512
ONES_ROWS = 16
BISECT_STEPS_PER_ROUND = 2
BISECT_MAX_ROUNDS = 17
SSD_TS = 256
SSD_L = 128
MERGE_TM = 512
MERGE_TN = 512
OUT_TM = 512
FFN_TM = 1024
FFN_TN = 512
DOWN_TM = 512
DOWN_TK = 1408

F32 = jnp.float32
BF16 = jnp.bfloat16
INT_MIN = -2147483648
NEG_BIG = -1e30


def _params(sem):
    return pltpu.CompilerParams(dimension_semantics=sem, vmem_limit_bytes=VMEM_LIMIT)


def _silu(x):
    return x * (1.0 / (1.0 + jnp.exp(-x)))


def _rms_body(x_ref, g_ref, o_ref):
    x = x_ref[...]
    y = x * lax.rsqrt(jnp.mean(x * x, axis=-1, keepdims=True) + NORM_EPS)
    o_ref[...] = (y * g_ref[...]).astype(o_ref.dtype)


def _rmsnorm(x2, g, tm=512):
    m, d = x2.shape
    return pl.pallas_call(
        _rms_body,
        grid=(m // tm,),
        in_specs=[pl.BlockSpec((tm, d), lambda i: (i, 0)), pl.BlockSpec((1, d), lambda i: (0, 0))],
        out_specs=pl.BlockSpec((tm, d), lambda i: (i, 0)),
        out_shape=jax.ShapeDtypeStruct((m, d), BF16),
        compiler_params=_params(("parallel",)),
        name="rmsnorm",
    )(x2, g.reshape(1, d))


def _rope_tables(seq, dim):
    pos = jnp.arange(seq, dtype=F32)
    inv = ROPE_THETA ** (-jnp.arange(0, dim, 2, dtype=F32) / dim)
    ang = pos[:, None] * inv[None, :]
    c, s = jnp.cos(ang), jnp.sin(ang)
    reps = LANES // dim
    return (jnp.tile(jnp.concatenate([c, c], axis=-1), (1, reps)),
            jnp.tile(jnp.concatenate([-s, s], axis=-1), (1, reps)))


def _swap_halves(x, dim):
    if dim == LANES:
        return pltpu.roll(x, LANES // 2, axis=1)
    lane = lax.broadcasted_iota(jnp.int32, x.shape, 1)
    first = (lane % dim) < (dim // 2)
    return jnp.where(first, pltpu.roll(x, LANES - dim // 2, axis=1), pltpu.roll(x, dim // 2, axis=1))


def _rope(x, cos_t, sin_t, dim):
    return x * cos_t + _swap_halves(x, dim) * sin_t


def _proj_call(u, w, epilogue, extras, extra_specs, out_shape, out_specs, tm, tn, name):
    bsz, seq, k = u.shape
    n = w.shape[1]

    def body(u_ref, w_ref, *refs):
        acc = jnp.dot(u_ref[0], w_ref[...], preferred_element_type=F32)
        epilogue(acc, *refs)

    return pl.pallas_call(
        body,
        grid=(bsz, seq // tm, n // tn),
        in_specs=[pl.BlockSpec((1, tm, k), lambda b, i, j: (b, i, 0)),
                  pl.BlockSpec((k, tn), lambda b, i, j: (0, j))] + list(extra_specs),
        out_specs=out_specs,
        out_shape=out_shape,
        compiler_params=_params(("parallel", "parallel", "arbitrary")),
        name=name,
    )(u, w, *extras)


def _table_spec(tm):
    return pl.BlockSpec((tm, LANES), lambda b, i, j: (i, 0))


def _rope_t_epilogue(dim, scale, acc, cos_ref, sin_ref, o_ref):
    c, s = cos_ref[...], sin_ref[...]
    for t in range(acc.shape[1] // LANES):
        sl = slice(t * LANES, (t + 1) * LANES)
        o_ref[0, sl, :] = (_rope(acc[:, sl], c, s, dim) * scale).T.astype(o_ref.dtype)


def _kv_epilogue(kw, acc, cos_ref, sin_ref, k_ref, vt_ref):
    c, s = cos_ref[...], sin_ref[...]
    for t in range(kw // LANES):
        sl = slice(t * LANES, (t + 1) * LANES)
        k_ref[0, :, sl] = _rope(acc[:, sl], c, s, HEAD_DIM).astype(k_ref.dtype)
        vt_ref[0, sl, :] = acc[:, kw + t * LANES:kw + (t + 1) * LANES].T.astype(vt_ref.dtype)


def _small_epilogue(n_idx, n_dt, acc, cos_ref, sin_ref, bias_ref, k2_ref, wt_ref, dt_ref):
    k2_ref[0] = _rope(acc[:, :LANES], cos_ref[...], sin_ref[...], IDX_DIM).astype(k2_ref.dtype)
    wt_ref[0] = acc[:, LANES:2 * LANES].T[:n_idx]
    x = acc[:, 2 * LANES:2 * LANES + n_dt] + bias_ref[...]
    dt_ref[0] = jnp.maximum(x, 0.0) + jnp.log1p(jnp.exp(-jnp.abs(x)))


def _plain_epilogue(acc, o_ref):
    o_ref[0] = acc.astype(o_ref.dtype)


def _to_key(x):
    bits = pltpu.bitcast(x, jnp.int32)
    return jnp.where(bits < 0, bits ^ jnp.int32(0x7FFFFFFF), bits)


def _dsa_body(qt_ref, qit_ref, wt_ref, k2_ref, k_ref, vt_ref, o_ref,
              keys_ref, slot_ref, rhs_ref, qb_ref, m_ref, acc_ref, s_ref, smax_ref, *, topk):
    tq = qt_ref.shape[2]
    n_idx = wt_ref.shape[1]
    seq = k_ref.shape[1]
    n_kv = k_ref.shape[2] // HEAD_DIM
    grp = qt_ref.shape[1] // HEAD_DIM // n_kv
    ir, cr, ar = DSA_IDX_ROWS, DSA_CNT_ROWS, DSA_ATT_ROWS
    q0 = pl.program_id(1) * tq
    n_ct = (q0 + tq + cr - 1) // cr
    n_at = (q0 + tq + ar - 1) // ar

    zero = jnp.zeros((IDX_DIM, tq), rhs_ref.dtype)
    for p in range(n_idx // 2):
        top = qit_ref[0, 2 * p * IDX_DIM:(2 * p + 1) * IDX_DIM, :]
        bot = qit_ref[0, (2 * p + 1) * IDX_DIM:(2 * p + 2) * IDX_DIM, :]
        rhs_ref[p] = jnp.concatenate([jnp.concatenate([top, zero], axis=1),
                                      jnp.concatenate([zero, bot], axis=1)], axis=0)

    qpos = q0 + lax.broadcasted_iota(jnp.int32, (ir, tq), 1)
    limit = (qpos // CHUNK + 1) * CHUNK
    krow = lax.broadcasted_iota(jnp.int32, (ir, tq), 0)

    def idx_tile(j, carry):
        for sub in range(cr // ir):
            k0 = pl.multiple_of(j * cr + sub * ir, ir)
            kblk = k2_ref[0, pl.ds(k0, ir), :]
            acc = jnp.zeros((ir, tq), F32)
            for p in range(n_idx // 2):
                d = jnp.dot(kblk, rhs_ref[p], preferred_element_type=F32)
                acc = acc + jnp.maximum(d[:, :tq], 0.0) * wt_ref[0, 2 * p:2 * p + 1, :]
                acc = acc + jnp.maximum(d[:, tq:], 0.0) * wt_ref[0, 2 * p + 1:2 * p + 2, :]
            key = jnp.where(krow + k0 < limit, _to_key(acc), INT_MIN)
            keys_ref[pl.ds(k0, ir), :] = key
            s0 = pl.multiple_of(lax.rem(k0, slots), ir)
            slot_ref[pl.ds(s0, ir), :] = jnp.maximum(slot_ref[pl.ds(s0, ir), :], key)
        return carry

    slots = slot_ref.shape[0]
    slot_ref[...] = jnp.full(slot_ref.shape, INT_MIN, jnp.int32)
    lax.fori_loop(0, n_ct, idx_tile, 0)

    def count(pred):
        def cnt_tile(j, cnt):
            k0 = pl.multiple_of(j * cr, cr)
            hit = jnp.where(pred(keys_ref[pl.ds(k0, cr), :], k0), 1, 0)
            return cnt + jnp.sum(hit.reshape(cr // 8, 8, tq), axis=0)

        cnt = lax.fori_loop(0, n_ct, cnt_tile, jnp.zeros((8, tq), jnp.int32))
        return jnp.sum(cnt, axis=0, keepdims=True)

    slot = slot_ref[...]
    lo = jnp.min(slot, axis=0, keepdims=True)
    hi = jnp.max(slot, axis=0, keepdims=True) + 1
    few = limit[:1, :] <= topk

    def settled(lo, hi, n_lo):
        ok = few | (n_lo == topk) | (hi - 1 <= lo)
        return jnp.min(jnp.where(ok, 1, 0))

    def bis_cond(carry):
        return (carry[3] < BISECT_MAX_ROUNDS) & (carry[4] == 0)

    def bis_round(carry):
        lo, hi, n_lo, rnd, _ = carry
        for _ in range(BISECT_STEPS_PER_ROUND):
            mid = (lo >> 1) + (hi >> 1) + (lo & hi & 1)
            total = count(lambda blk, k0: blk >= mid)
            ok = total >= topk
            lo, hi, n_lo = jnp.where(ok, mid, lo), jnp.where(ok, hi, mid), jnp.where(ok, total, n_lo)
        return lo, hi, n_lo, rnd + 1, settled(lo, hi, n_lo)

    n_lo = jnp.full((1, tq), topk + 1, jnp.int32)
    lo, hi, n_lo, _, _ = lax.while_loop(bis_cond, bis_round, (lo, hi, n_lo, jnp.int32(0), settled(lo, hi, n_lo)))
    tau = jnp.where(few, INT_MIN + 1, jnp.maximum(lo, INT_MIN + 1))
    n_ge = jnp.where(few, 0, count(lambda blk, k0: blk >= tau))

    @pl.when(jnp.max(n_ge) > topk)
    def _():
        need = topk - count(lambda blk, k0: blk > tau)
        pos = lax.broadcasted_iota(jnp.int32, (cr, tq), 0)
        pos_bits = max(1, (seq - 1).bit_length())

        def pos_iter(it, last):
            cand = last + jnp.left_shift(jnp.int32(1), pos_bits - 1 - it)
            below = count(lambda blk, k0: jnp.where(blk == tau, pos + k0, cand) < cand)
            return jnp.where(below < need, cand, last)

        last = lax.fori_loop(0, pos_bits, pos_iter, jnp.zeros((1, tq), jnp.int32))

        def demote(j, carry):
            k0 = pl.multiple_of(j * cr, cr)
            blk = keys_ref[pl.ds(k0, cr), :]
            drop = jnp.where(blk == tau, pos + k0, last) > last
            keys_ref[pl.ds(k0, cr), :] = jnp.where(drop, tau - 1, blk)
            return carry

        lax.fori_loop(0, n_ct, demote, 0)

    m_ref[...] = jnp.full(m_ref.shape, NEG_BIG, F32)
    acc_ref[...] = jnp.zeros(acc_ref.shape, F32)

    eye = (lax.broadcasted_iota(jnp.int32, (tq, tq), 0) == lax.broadcasted_iota(jnp.int32, (tq, tq), 1))
    eye = jnp.concatenate([jnp.where(eye, 1.0, 0.0).astype(BF16)] * grp, axis=1)
    for g in range(n_kv):
        qtg = jnp.concatenate([qt_ref[0, (g * grp + i) * HEAD_DIM:(g * grp + i + 1) * HEAD_DIM, :]
                               for i in range(grp)], axis=1)
        qb_ref[g] = jnp.concatenate([qtg, eye], axis=0)
    ones = jnp.ones((ONES_ROWS, ar), BF16)

    def logits_stage(j, slot):
        k0 = pl.multiple_of(j * ar, ar)
        bias = jnp.where(keys_ref[pl.ds(k0, ar), :] >= tau, 0.0, NEG_BIG).astype(BF16)
        for g in range(n_kv):
            kg = k_ref[0, pl.ds(k0, ar), g * HEAD_DIM:(g + 1) * HEAD_DIM]
            s = jnp.dot(jnp.concatenate([kg, bias], axis=1), qb_ref[g], preferred_element_type=F32)
            s_ref[slot, g] = s
            smax_ref[slot, g] = jnp.max(s, axis=0, keepdims=True)

    def softmax_stage(j, slot):
        k0 = pl.multiple_of(j * ar, ar)
        for g in range(n_kv):
            m_old = m_ref[g]
            m_new = jnp.maximum(m_old, smax_ref[slot, g])
            alpha = jnp.exp2(m_old - m_new)
            p = jnp.exp2(s_ref[slot, g] - m_new).astype(BF16)
            vtg = jnp.concatenate([vt_ref[0, g * HEAD_DIM:(g + 1) * HEAD_DIM, pl.ds(k0, ar)], ones], axis=0)
            acc_ref[g] = alpha * acc_ref[g] + jnp.dot(vtg, p, preferred_element_type=F32)
            m_ref[g] = m_new

    logits_stage(0, 0)

    def att_tile(j, carry):
        slot = j % 2
        softmax_stage(j, slot)
        logits_stage(jnp.minimum(j + 1, n_at - 1), 1 - slot)
        return carry

    lax.fori_loop(0, n_at, att_tile, 0)

    for g in range(n_kv):
        o = acc_ref[g, :HEAD_DIM, :] / acc_ref[g, HEAD_DIM:HEAD_DIM + 1, :]
        for i in range(grp):
            h = g * grp + i
            o_ref[0, :, h * HEAD_DIM:(h + 1) * HEAD_DIM] = o[:, i * tq:(i + 1) * tq].T.astype(o_ref.dtype)


def _dsa(qt, qit, wt, k2, k, vt, topk):
    bsz, aw, seq = qt.shape
    tq = LANES
    n_idx = wt.shape[1]
    kw = k.shape[2]
    n_kv = kw // HEAD_DIM
    grp = aw // kw
    once = pl.Buffered(1)
    return pl.pallas_call(
        functools.partial(_dsa_body, topk=topk),
        grid=(bsz, seq // tq),
        in_specs=[
            pl.BlockSpec((1, aw, tq), lambda b, i: (b, 0, i)),
            pl.BlockSpec((1, qit.shape[1], tq), lambda b, i: (b, 0, i)),
            pl.BlockSpec((1, n_idx, tq), lambda b, i: (b, 0, i)),
            pl.BlockSpec((1, seq, 2 * IDX_DIM), lambda b, i: (b, 0, 0), pipeline_mode=once),
            pl.BlockSpec((1, seq, kw), lambda b, i: (b, 0, 0), pipeline_mode=once),
            pl.BlockSpec((1, kw, seq), lambda b, i: (b, 0, 0), pipeline_mode=once),
        ],
        out_specs=pl.BlockSpec((1, tq, aw), lambda b, i: (b, i, 0)),
        out_shape=jax.ShapeDtypeStruct((bsz, seq, aw), BF16),
        scratch_shapes=[
            pltpu.VMEM((seq, tq), jnp.int32),
            pltpu.VMEM((-(-topk // DSA_IDX_ROWS) * DSA_IDX_ROWS, tq), jnp.int32),
            pltpu.VMEM((n_idx // 2, 2 * IDX_DIM, 2 * tq), BF16),
            pltpu.VMEM((n_kv, HEAD_DIM + tq, grp * tq), BF16),
            pltpu.VMEM((n_kv, 1, grp * tq), F32),
            pltpu.VMEM((n_kv, HEAD_DIM + ONES_ROWS, grp * tq), F32),
            pltpu.VMEM((2, n_kv, DSA_ATT_ROWS, grp * tq), F32),
            pltpu.VMEM((2, n_kv, 1, grp * tq), F32),
        ],
        compiler_params=_params(("parallel", "arbitrary")),
        name="dsa",
    )(qt, qit, wt, k2, k, vt)


def _conv_silu(blk_ref, halo_ref, buf_ref, w_ref, b_ref, ts):
    blk = blk_ref[0]
    buf_ref[0:8, :] = halo_ref[...]
    buf_ref[8:8 + ts, :] = blk
    halo_ref[...] = blk[ts - 8:ts, :]
    out = b_ref[...] + w_ref[CONV_WIDTH - 1:CONV_WIDTH, :] * blk
    for k in range(CONV_WIDTH - 1):
        out = out + w_ref[k:k + 1, :] * buf_ref[pl.ds(8 - (CONV_WIDTH - 1) + k, ts), :]
    return _silu(out)


def _ssd_body(z_ref, x_ref, b_ref, c_ref, dt_ref, dtt_ref, alog_row_ref, alog_col_ref, dskip_ref, gn_ref,
              wx_ref, wb_ref, wc_ref, bx_ref, bb_ref, bc_ref, o_ref,
              ht_ref, hx_ref, hb_ref, hc_ref, bufx_ref, bufb_ref, bufc_ref):
    ts = x_ref.shape[1]
    hg = dt_ref.shape[3]
    hd = SSM_HEAD_DIM
    ll = SSD_L

    @pl.when(pl.program_id(2) == 0)
    def _():
        ht_ref[...] = jnp.zeros(ht_ref.shape, F32)
        hx_ref[...] = jnp.zeros(hx_ref.shape, F32)
        hb_ref[...] = jnp.zeros(hb_ref.shape, F32)
        hc_ref[...] = jnp.zeros(hc_ref.shape, F32)

    xs = _conv_silu(x_ref, hx_ref, bufx_ref, wx_ref, bx_ref, ts)
    bm = _conv_silu(b_ref, hb_ref, bufb_ref, wb_ref, bb_ref, ts)
    cm = _conv_silu(c_ref, hc_ref, bufc_ref, wc_ref, bc_ref, ts)

    a_row = -jnp.exp(alog_row_ref[0])
    a_col = -jnp.exp(alog_col_ref[0])
    ti = lax.broadcasted_iota(jnp.int32, (ll, ll), 0)
    si = lax.broadcasted_iota(jnp.int32, (ll, ll), 1)
    causal = si <= ti
    tril = jnp.where(causal, 1.0, 0.0).astype(F32)
    triu = jnp.where(ti <= si, 1.0, 0.0).astype(F32)
    lane = lax.broadcasted_iota(jnp.int32, (1, 2 * hd), 1)
    first = lane < hd

    for ci in range(ts // ll):
        rows = slice(ci * ll, (ci + 1) * ll)
        dt = dt_ref[0, 0, rows, :]
        dtt = dtt_ref[0, 0, :, rows]
        acol = jnp.dot(tril, dt * a_row, preferred_element_type=F32, precision=lax.Precision.HIGHEST)
        arow = jnp.dot(dtt * a_col, triu, preferred_element_type=F32, precision=lax.Precision.HIGHEST)
        alast = arow[:, ll - 1:ll]
        wrow = dtt * jnp.exp(alast - arow)
        elast = jnp.exp(alast)
        xc = xs[rows]
        bc = bm[rows]
        cc = cm[rows]
        bcb = bc.astype(BF16)
        ccb = cc.astype(BF16)
        cb = lax.dot_general(ccb, bcb, (((1,), (1,)), ((), ())), preferred_element_type=F32)
        bt = bc.T
        ys = []
        for p in range(hg // 2):
            xp = xc[:, p * 2 * hd:(p + 1) * 2 * hd].astype(BF16)
            hp = ht_ref[p]
            rhs = jnp.concatenate([xp, hp.astype(BF16)], axis=0)
            y2, s2, e2 = [], [], []
            for e in range(2):
                j = 2 * p + e
                bcol = jnp.broadcast_to(acol[:, j:j + 1], (ll, ll))
                dec = jnp.where(causal, jnp.exp(bcol - arow[j:j + 1, :]), 0.0)
                mm = cb * dec * dtt[j:j + 1, :]
                ecol = jnp.exp(jnp.broadcast_to(acol[:, j:j + 1], (ll, SSM_STATE)))
                lhs = jnp.concatenate([mm, cc * ecol], axis=1).astype(BF16)
                y2.append(jnp.dot(lhs, rhs, preferred_element_type=F32))
                s2.append(jnp.dot((bt * wrow[j:j + 1, :]).astype(BF16), xp, preferred_element_type=F32))
                e2.append(jnp.broadcast_to(elast[j:j + 1, :], (1, 2 * hd)))
            ys.append(jnp.where(first, y2[0], y2[1]))
            ht_ref[p] = hp * jnp.where(first, e2[0], e2[1]) + jnp.where(first, s2[0], s2[1])
        y = jnp.concatenate(ys, axis=1)
        y = (y + dskip_ref[0] * xc) * _silu(z_ref[0, rows, :])
        y = y * lax.rsqrt(jnp.mean(y * y, axis=-1, keepdims=True) + NORM_EPS)
        o_ref[0, rows, :] = (y * gn_ref[0]).astype(o_ref.dtype)


def _ssd(zx, dt, conv_w, conv_b, a_log, d_skip, g_norm):
    bsz, seq, _ = zx.shape
    heads = dt.shape[2]
    g = SSM_GROUPS
    hg = heads // g
    wid = heads * SSM_HEAD_DIM
    gw = wid // g
    n = SSM_STATE
    ts = SSD_TS
    dtg = dt.reshape(bsz, seq, g, hg).transpose(0, 2, 1, 3)
    dtt = dtg.transpose(0, 1, 3, 2)
    zb, xb, bb, cb = 0, wid // gw, 2 * wid // n, (2 * wid + g * n) // n
    cw_x, cw_b, cw_c = conv_w[:, :wid], conv_w[:, wid:wid + g * n], conv_w[:, wid + g * n:]
    cb_x, cb_b, cb_c = (conv_b[None, :wid], conv_b[None, wid:wid + g * n], conv_b[None, wid + g * n:])
    col = lambda off: (lambda b, gi, c: (b, c, off + gi))
    par = lambda b, gi, c: (0, gi)
    return pl.pallas_call(
        _ssd_body,
        grid=(bsz, g, seq // ts),
        in_specs=[
            pl.BlockSpec((1, ts, gw), col(zb)),
            pl.BlockSpec((1, ts, gw), col(xb)),
            pl.BlockSpec((1, ts, n), col(bb)),
            pl.BlockSpec((1, ts, n), col(cb)),
            pl.BlockSpec((1, 1, ts, hg), lambda b, gi, c: (b, gi, c, 0)),
            pl.BlockSpec((1, 1, hg, ts), lambda b, gi, c: (b, gi, 0, c)),
            pl.BlockSpec((1, 1, hg), lambda b, gi, c: (gi, 0, 0)),
            pl.BlockSpec((1, hg, 1), lambda b, gi, c: (gi, 0, 0)),
            pl.BlockSpec((1, 1, gw), lambda b, gi, c: (gi, 0, 0)),
            pl.BlockSpec((1, 1, gw), lambda b, gi, c: (gi, 0, 0)),
            pl.BlockSpec((CONV_WIDTH, gw), par),
            pl.BlockSpec((CONV_WIDTH, n), par),
            pl.BlockSpec((CONV_WIDTH, n), par),
            pl.BlockSpec((1, gw), par),
            pl.BlockSpec((1, n), par),
            pl.BlockSpec((1, n), par),
        ],
        out_specs=pl.BlockSpec((1, ts, gw), lambda b, gi, c: (b, c, gi)),
        out_shape=jax.ShapeDtypeStruct((bsz, seq, wid), BF16),
        scratch_shapes=[
            pltpu.VMEM((hg // 2, n, 2 * SSM_HEAD_DIM), F32),
            pltpu.VMEM((8, gw), F32), pltpu.VMEM((8, n), F32), pltpu.VMEM((8, n), F32),
            pltpu.VMEM((ts + 8, gw), F32), pltpu.VMEM((ts + 8, n), F32), pltpu.VMEM((ts + 8, n), F32),
        ],
        compiler_params=_params(("parallel", "parallel", "arbitrary")),
        name="ssd",
    )(zx, zx, zx, zx, dtg, dtt, a_log.reshape(g, 1, hg), a_log.reshape(g, hg, 1),
      jnp.repeat(d_skip, SSM_HEAD_DIM).reshape(g, 1, gw), g_norm.reshape(g, 1, gw),
      cw_x, cw_b, cw_c, cb_x, cb_b, cb_c)


def _merge_body(u_ref, ya_ref, ys_ref, wg0_ref, wg1_ref, wa_ref, ws_ref, o_ref):
    u = u_ref[...]
    g0 = jax.nn.sigmoid(jnp.dot(u, wg0_ref[...], preferred_element_type=F32))
    g1 = jax.nn.sigmoid(jnp.dot(u, wg1_ref[...], preferred_element_type=F32))
    a = jnp.dot(ya_ref[...], wa_ref[...], preferred_element_type=F32)
    s = jnp.dot(ys_ref[...], ws_ref[...], preferred_element_type=F32)
    o_ref[...] = (g0 * a + g1 * s).astype(o_ref.dtype)


def _merge(u, ya, ys, w_gate, w_a, w_s):
    m, d = u.shape
    tm, tn = MERGE_TM, MERGE_TN
    nj = d // tn
    return pl.pallas_call(
        _merge_body,
        grid=(m // tm, nj),
        in_specs=[
            pl.BlockSpec((tm, d), lambda i, j: (i, 0)),
            pl.BlockSpec((tm, ya.shape[1]), lambda i, j: (i, 0)),
            pl.BlockSpec((tm, ys.shape[1]), lambda i, j: (i, 0)),
            pl.BlockSpec((d, tn), lambda i, j: (0, j)),
            pl.BlockSpec((d, tn), lambda i, j: (0, nj + j)),
            pl.BlockSpec((w_a.shape[0], tn), lambda i, j: (0, j)),
            pl.BlockSpec((w_s.shape[0], tn), lambda i, j: (0, j)),
        ],
        out_specs=pl.BlockSpec((tm, tn), lambda i, j: (i, j)),
        out_shape=jax.ShapeDtypeStruct((m, d), BF16),
        compiler_params=_params(("parallel", "arbitrary")),
        name="merge",
    )(u, ya, ys, w_gate, w_gate, w_a, w_s)


def _out_body(mg_ref, x_ref, w_ref, g_ref, h_ref, u_ref):
    h = x_ref[...] + jnp.dot(mg_ref[...], w_ref[...], preferred_element_type=F32)
    h_ref[...] = h
    y = h * lax.rsqrt(jnp.mean(h * h, axis=-1, keepdims=True) + NORM_EPS)
    u_ref[...] = (y * g_ref[...]).astype(u_ref.dtype)


def _out_proj(merged, x2, w_out, g_ffn):
    m, d = x2.shape
    tm = OUT_TM
    row = pl.BlockSpec((tm, d), lambda i: (i, 0))
    return pl.pallas_call(
        _out_body,
        grid=(m // tm,),
        in_specs=[row, row, pl.BlockSpec((d, d), lambda i: (0, 0)), pl.BlockSpec((1, d), lambda i: (0, 0))],
        out_specs=[row, row],
        out_shape=[jax.ShapeDtypeStruct((m, d), F32), jax.ShapeDtypeStruct((m, d), BF16)],
        compiler_params=_params(("parallel",)),
        name="out_proj",
    )(merged, x2, w_out, g_ffn.reshape(1, d))


def _ffn_up_body(u_ref, wg_ref, wu_ref, o_ref):
    u = u_ref[...]
    gate = jnp.dot(u, wg_ref[...], preferred_element_type=F32)
    up = jnp.dot(u, wu_ref[...], preferred_element_type=F32)
    o_ref[...] = (_silu(gate) * up).astype(o_ref.dtype)


def _ffn_up(u2, w_in):
    m, d = u2.shape
    dff = w_in.shape[1] // 2
    tm, tn = FFN_TM, FFN_TN
    nj = dff // tn
    return pl.pallas_call(
        _ffn_up_body,
        grid=(m // tm, nj),
        in_specs=[pl.BlockSpec((tm, d), lambda i, j: (i, 0)),
                  pl.BlockSpec((d, tn), lambda i, j: (0, j)),
                  pl.BlockSpec((d, tn), lambda i, j: (0, nj + j))],
        out_specs=pl.BlockSpec((tm, tn), lambda i, j: (i, j)),
        out_shape=jax.ShapeDtypeStruct((m, dff), BF16),
        compiler_params=_params(("parallel", "arbitrary")),
        name="ffn_up",
    )(u2, w_in, w_in)


def _ffn_down_body(a_ref, w_ref, h_ref, g_ref, o_ref, acc_ref):
    k = pl.program_id(1)

    @pl.when(k == 0)
    def _():
        acc_ref[...] = h_ref[...]

    acc_ref[...] += jnp.dot(a_ref[...], w_ref[...], preferred_element_type=F32)

    @pl.when(k == pl.num_programs(1) - 1)
    def _():
        h = acc_ref[...]
        y = h * lax.rsqrt(jnp.mean(h * h, axis=-1, keepdims=True) + NORM_EPS)
        o_ref[...] = y * g_ref[...]


def _ffn_down(act, w_out, h1, g_final):
    m, dff = act.shape
    d = w_out.shape[1]
    tm, tk = DOWN_TM, DOWN_TK
    return pl.pallas_call(
        _ffn_down_body,
        grid=(m // tm, dff // tk),
        in_specs=[pl.BlockSpec((tm, tk), lambda i, k: (i, k)),
                  pl.BlockSpec((tk, d), lambda i, k: (k, 0)),
                  pl.BlockSpec((tm, d), lambda i, k: (i, 0)),
                  pl.BlockSpec((1, d), lambda i, k: (0, 0))],
        out_specs=pl.BlockSpec((tm, d), lambda i, k: (i, 0)),
        out_shape=jax.ShapeDtypeStruct((m, d), F32),
        scratch_shapes=[pltpu.VMEM((tm, d), F32)],
        compiler_params=_params(("parallel", "arbitrary")),
        name="ffn_down",
    )(act, w_out, h1, g_final.reshape(1, d))


def _layer(h, w_in, w_gate, w_attn_branch, w_ssm_branch, w_out, conv_w, conv_b, dt_bias, a_log, d_skip,
           g_ssm_norm, g_mix, g_ffn, w_ffn_in, w_ffn_out, g_last):
    bsz, seq, d = h.shape
    m = bsz * seq
    aw = N_HEADS * HEAD_DIM
    kw = N_KV_HEADS * HEAD_DIM
    iw = N_IDX_HEADS * IDX_DIM
    sw = SSM_EXPAND * d
    heads = sw // SSM_HEAD_DIM
    bcw = SSM_GROUPS * SSM_STATE
    cuts = [0]
    for s in (aw, kw, kw, iw, IDX_DIM, N_IDX_HEADS, sw, sw + 2 * bcw, heads):
        cuts.append(cuts[-1] + s)
    wb = w_in.astype(BF16)
    seg = lambda i, j=None: wb[:, cuts[i]:cuts[i + 1 if j is None else j]]
    pad = lambda a, n: jnp.pad(a, ((0, 0), (0, n - a.shape[1])))
    w_small = jnp.concatenate([seg(4), seg(4), pad(seg(5), LANES), pad(seg(8), LANES)], axis=1)

    x2 = h.reshape(m, d)
    u = _rmsnorm(x2, g_mix)
    u3 = u.reshape(bsz, seq, d)

    tm = min(PROJ_TM, seq)
    cos_a, sin_a = _rope_tables(seq, HEAD_DIM)
    cos_i, sin_i = _rope_tables(seq, IDX_DIM)
    tab = _table_spec(tm)
    blk = lambda n: pl.BlockSpec((1, tm, n), lambda b, i, j: (b, i, j))
    blk_t = lambda n: pl.BlockSpec((1, n, tm), lambda b, i, j: (b, j, i))

    tn = min(PROJ_TN, aw)
    qt = _proj_call(u3, seg(0), functools.partial(_rope_t_epilogue, HEAD_DIM, HEAD_DIM ** -0.5 * math.log2(math.e)),
                    (cos_a, sin_a), (tab, tab), jax.ShapeDtypeStruct((bsz, aw, seq), BF16), blk_t(tn), tm, tn,
                    "proj_q")
    k, vt = _proj_call(u3, seg(1, 3), functools.partial(_kv_epilogue, kw), (cos_a, sin_a), (tab, tab),
                       [jax.ShapeDtypeStruct((bsz, seq, kw), BF16), jax.ShapeDtypeStruct((bsz, kw, seq), BF16)],
                       [pl.BlockSpec((1, tm, kw), lambda b, i, j: (b, i, 0)),
                        pl.BlockSpec((1, kw, tm), lambda b, i, j: (b, 0, i))], tm, 2 * kw, "proj_kv")
    tn = min(PROJ_TN, iw)
    qit = _proj_call(u3, seg(3), functools.partial(_rope_t_epilogue, IDX_DIM, 1.0),
                     (cos_i, sin_i), (tab, tab), jax.ShapeDtypeStruct((bsz, iw, seq), BF16), blk_t(tn), tm, tn,
                     "proj_qidx")
    k2, wt, dt = _proj_call(
        u3, w_small, functools.partial(_small_epilogue, N_IDX_HEADS, heads),
        (cos_i, sin_i, dt_bias.reshape(1, heads)),
        (tab, tab, pl.BlockSpec((1, heads), lambda b, i, j: (0, 0))),
        [jax.ShapeDtypeStruct((bsz, seq, 2 * IDX_DIM), BF16), jax.ShapeDtypeStruct((bsz, N_IDX_HEADS, seq), F32),
         jax.ShapeDtypeStruct((bsz, seq, heads), F32)],
        [pl.BlockSpec((1, tm, 2 * IDX_DIM), lambda b, i, j: (b, i, 0)),
         pl.BlockSpec((1, N_IDX_HEADS, tm), lambda b, i, j: (b, 0, i)),
         pl.BlockSpec((1, tm, heads), lambda b, i, j: (b, i, 0))], tm, 3 * LANES, "proj_small")
    nzx = 2 * sw + 2 * bcw
    tn = min(PROJ_TN, nzx)
    zx = _proj_call(u3, seg(6, 8), _plain_epilogue, (), (), jax.ShapeDtypeStruct((bsz, seq, nzx), F32),
                    blk(tn), tm, tn, "proj_zx")

    y_attn = _dsa(qt, qit, wt, k2, k, vt, min(TOPK_MAX, seq // 4))
    y_ssm = _ssd(zx, dt, conv_w, conv_b, a_log, d_skip, g_ssm_norm)

    merged = _merge(u, y_attn.reshape(m, aw), y_ssm.reshape(m, sw), w_gate.astype(BF16),
                    w_attn_branch.astype(BF16), w_ssm_branch.astype(BF16))
    h1, u2 = _out_proj(merged, x2, w_out.astype(BF16), g_ffn)
    act = _ffn_up(u2, w_ffn_in.astype(BF16))
    return _ffn_down(act, w_ffn_out.astype(BF16), h1, g_last).reshape(bsz, seq, d)


def kernel(x, w_in, w_gate, w_attn_branch, w_ssm_branch, w_out, conv_w, conv_b, dt_bias, a_log, d_skip, g_ssm_norm,
           g_mix, g_ffn, w_ffn_in, w_ffn_out, g_final):
    depth = w_in.shape[0]
    assert depth == 1, "the final norm is fused into the last layer's down projection"
    return _layer(x, w_in[0], w_gate[0], w_attn_branch[0], w_ssm_branch[0], w_out[0], conv_w[0], conv_b[0],
                  dt_bias[0], a_log[0], d_skip[0], g_ssm_norm[0], g_mix[0], g_ffn[0], w_ffn_in[0], w_ffn_out[0],
                  g_final)
```

```python
import functools
import math

import jax
import jax.numpy as jnp
from jax import lax
from jax.experimental import pallas as pl
from jax.experimental.pallas import tpu as pltpu

CHUNK = 64
ROPE_THETA = 10000.0
NORM_EPS = 1e-6
N_HEADS = 16
N_KV_HEADS = 4
HEAD_DIM = 128
TOPK_MAX = 256
N_IDX_HEADS = 16
IDX_DIM = 64
SSM_EXPAND = 2
SSM_HEAD_DIM = 64
SSM_GROUPS = 8
SSM_STATE = 128
CONV_WIDTH = 4

LANES = 128
VMEM_LIMIT = 60 * 1024 * 1024

PROJ_TM = 1024
PROJ_TN = 1024
DSA_IDX_ROWS = 128
DSA_STEP_ROWS = 512
DSA_CNT_ROWS = 1024
DSA_ATT_ROWS = 256
CNT_ACC_ROWS = 32
ONES_ROWS = 16
BISECT_STEPS_PER_ROUND = 2
BISECT_MAX_ROUNDS = 17
SSD_TS = 256
SSD_L = 128
MERGE_TM = 512
MERGE_TN = 512
OUT_TM = 512
FFN_TM = 1024
FFN_TN = 512
DOWN_TM = 512

F32 = jnp.float32
BF16 = jnp.bfloat16
INT_MIN = -2147483648
NEG_BIG = -1e30


def _params(sem):
    return pltpu.CompilerParams(dimension_semantics=sem, vmem_limit_bytes=VMEM_LIMIT)


def _silu(x):
    h = 0.5 * x
    return h + h * jnp.tanh(h)


def _rms_body(x_ref, g_ref, o_ref):
    x = x_ref[...]
    y = x * lax.rsqrt(jnp.mean(x * x, axis=-1, keepdims=True) + NORM_EPS)
    o_ref[...] = (y * g_ref[...]).astype(o_ref.dtype)


def _rmsnorm(x2, g, tm=512):
    m, d = x2.shape
    return pl.pallas_call(
        _rms_body,
        grid=(m // tm,),
        in_specs=[pl.BlockSpec((tm, d), lambda i: (i, 0)), pl.BlockSpec((1, d), lambda i: (0, 0))],
        out_specs=pl.BlockSpec((tm, d), lambda i: (i, 0)),
        out_shape=jax.ShapeDtypeStruct((m, d), BF16),
        compiler_params=_params(("parallel",)),
        name="rmsnorm",
    )(x2, g.reshape(1, d))


def _rope_tables(seq, dim):
    pos = jnp.arange(seq, dtype=F32)
    inv = ROPE_THETA ** (-jnp.arange(0, dim, 2, dtype=F32) / dim)
    ang = pos[:, None] * inv[None, :]
    c, s = jnp.cos(ang), jnp.sin(ang)
    reps = LANES // dim
    return (jnp.tile(jnp.concatenate([c, c], axis=-1), (1, reps)),
            jnp.tile(jnp.concatenate([-s, s], axis=-1), (1, reps)))


def _swap_halves(x, dim):
    if dim == LANES:
        return pltpu.roll(x, LANES // 2, axis=1)
    lane = lax.broadcasted_iota(jnp.int32, x.shape, 1)
    first = (lane % dim) < (dim // 2)
    return jnp.where(first, pltpu.roll(x, LANES - dim // 2, axis=1), pltpu.roll(x, dim // 2, axis=1))


def _rope(x, cos_t, sin_t, dim):
    return x * cos_t + _swap_halves(x, dim) * sin_t


def _proj_call(u, w, epilogue, extras, extra_specs, out_shape, out_specs, tm, tn, name):
    bsz, seq, k = u.shape
    n = w.shape[1]

    def body(u_ref, w_ref, *refs):
        acc = jnp.dot(u_ref[0], w_ref[...], preferred_element_type=F32)
        epilogue(acc, *refs)

    return pl.pallas_call(
        body,
        grid=(bsz, seq // tm, n // tn),
        in_specs=[pl.BlockSpec((1, tm, k), lambda b, i, j: (b, i, 0)),
                  pl.BlockSpec((k, tn), lambda b, i, j: (0, j))] + list(extra_specs),
        out_specs=out_specs,
        out_shape=out_shape,
        compiler_params=_params(("parallel", "parallel", "arbitrary")),
        name=name,
    )(u, w, *extras)


def _table_spec(tm):
    return pl.BlockSpec((tm, LANES), lambda b, i, j: (i, 0))


def _rope_t_epilogue(dim, scale, acc, cos_ref, sin_ref, o_ref):
    c, s = cos_ref[...], sin_ref[...]
    for t in range(acc.shape[1] // LANES):
        sl = slice(t * LANES, (t + 1) * LANES)
        o_ref[0, sl, :] = (_rope(acc[:, sl], c, s, dim) * scale).T.astype(o_ref.dtype)


def _kv_epilogue(kw, acc, cos_ref, sin_ref, k_ref, vt_ref):
    c, s = cos_ref[...], sin_ref[...]
    for t in range(kw // LANES):
        sl = slice(t * LANES, (t + 1) * LANES)
        k_ref[0, :, sl] = _rope(acc[:, sl], c, s, HEAD_DIM).astype(k_ref.dtype)
        vt_ref[0, sl, :] = acc[:, kw + t * LANES:kw + (t + 1) * LANES].T.astype(vt_ref.dtype)


def _small_epilogue(n_idx, n_dt, acc, cos_ref, sin_ref, bias_ref, k2_ref, wt_ref, dt_ref):
    k2_ref[0] = _rope(acc[:, :LANES], cos_ref[...], sin_ref[...], IDX_DIM).astype(k2_ref.dtype)
    wt_ref[0] = acc[:, LANES:2 * LANES].T[:n_idx]
    x = acc[:, 2 * LANES:2 * LANES + n_dt] + bias_ref[...]
    dt_ref[0] = jnp.maximum(x, 0.0) + jnp.log1p(jnp.exp(-jnp.abs(x)))


def _plain_epilogue(acc, o_ref):
    o_ref[0] = acc.astype(o_ref.dtype)


def _to_key(x):
    bits = pltpu.bitcast(x, jnp.int32)
    return jnp.where(bits < 0, bits ^ jnp.int32(0x7FFFFFFF), bits)


def _dsa_body(qt_ref, qit_ref, wt_ref, k2_ref, k_ref, vt_ref, o_ref,
              keys_ref, slot_ref, rhs_ref, qb_ref, m_ref, acc_ref, s_ref, smax_ref, *, topk):
    tq = qt_ref.shape[2]
    n_idx = wt_ref.shape[1]
    seq = k_ref.shape[1]
    n_kv = k_ref.shape[2] // HEAD_DIM
    grp = qt_ref.shape[1] // HEAD_DIM // n_kv
    ir, ar = DSA_IDX_ROWS, DSA_ATT_ROWS
    cr = min(DSA_CNT_ROWS, seq)
    st = min(DSA_STEP_ROWS, cr)
    q0 = pl.program_id(1) * tq
    n_ct = (q0 + tq + cr - 1) // cr
    n_at = (q0 + tq + ar - 1) // ar

    zero = jnp.zeros((IDX_DIM, tq), rhs_ref.dtype)
    for p in range(n_idx // 2):
        top = qit_ref[0, 2 * p * IDX_DIM:(2 * p + 1) * IDX_DIM, :]
        bot = qit_ref[0, (2 * p + 1) * IDX_DIM:(2 * p + 2) * IDX_DIM, :]
        rhs_ref[p] = jnp.concatenate([jnp.concatenate([top, zero], axis=1),
                                      jnp.concatenate([zero, bot], axis=1)], axis=0)

    qpos = q0 + lax.broadcasted_iota(jnp.int32, (ir, tq), 1)
    limit = (qpos // CHUNK + 1) * CHUNK
    krow = lax.broadcasted_iota(jnp.int32, (ir, tq), 0)

    def idx_tile(j, carry):
        for sub in range(st // ir):
            k0 = pl.multiple_of(j * st + sub * ir, ir)
            kblk = k2_ref[0, pl.ds(k0, ir), :]
            acc = jnp.zeros((ir, tq), F32)
            for p in range(n_idx // 2):
                d = jnp.dot(kblk, rhs_ref[p], preferred_element_type=F32)
                acc = acc + jnp.maximum(d[:, :tq], 0.0) * wt_ref[0, 2 * p:2 * p + 1, :]
                acc = acc + jnp.maximum(d[:, tq:], 0.0) * wt_ref[0, 2 * p + 1:2 * p + 2, :]
            key = jnp.where(krow + k0 < limit, _to_key(acc), INT_MIN)
            keys_ref[pl.ds(k0, ir), :] = key
            s0 = pl.multiple_of(lax.rem(k0, slots), ir)
            slot_ref[pl.ds(s0, ir), :] = jnp.maximum(slot_ref[pl.ds(s0, ir), :], key)
        return carry

    slots = slot_ref.shape[0]
    slot_ref[...] = jnp.full(slot_ref.shape, INT_MIN, jnp.int32)
    lax.fori_loop(0, n_ct * (cr // st), idx_tile, 0)

    def count(pred):
        def cnt_tile(j, cnt):
            k0 = pl.multiple_of(j * cr, cr)
            hit = jnp.where(pred(keys_ref[pl.ds(k0, cr), :], k0), 1, 0)
            return cnt + jnp.sum(hit.reshape(cr // CNT_ACC_ROWS, CNT_ACC_ROWS, tq), axis=0)

        cnt = lax.fori_loop(0, n_ct, cnt_tile, jnp.zeros((CNT_ACC_ROWS, tq), jnp.int32))
        return jnp.sum(cnt, axis=0, keepdims=True)

    slot = slot_ref[...]
    lo = jnp.min(slot, axis=0, keepdims=True)
    hi = jnp.max(slot, axis=0, keepdims=True) + 1
    few = limit[:1, :] <= topk

    def settled(lo, hi, n_lo):
        ok = few | (n_lo == topk) | (hi - 1 <= lo)
        return jnp.min(jnp.where(ok, 1, 0))

    def bis_cond(carry):
        return (carry[3] < BISECT_MAX_ROUNDS) & (carry[4] == 0)

    def bis_round(carry):
        lo, hi, n_lo, rnd, _ = carry
        for _ in range(BISECT_STEPS_PER_ROUND):
            mid = (lo >> 1) + (hi >> 1) + (lo & hi & 1)
            total = count(lambda blk, k0: blk >= mid)
            ok = total >= topk
            lo, hi, n_lo = jnp.where(ok, mid, lo), jnp.where(ok, hi, mid), jnp.where(ok, total, n_lo)
        return lo, hi, n_lo, rnd + 1, settled(lo, hi, n_lo)

    n_lo = jnp.full((1, tq), topk + 1, jnp.int32)
    lo, hi, n_lo, _, _ = lax.while_loop(bis_cond, bis_round, (lo, hi, n_lo, jnp.int32(0), settled(lo, hi, n_lo)))
    tau = jnp.where(few, INT_MIN + 1, jnp.maximum(lo, INT_MIN + 1))
    n_ge = jnp.where(few, 0, count(lambda blk, k0: blk >= tau))

    @pl.when(jnp.max(n_ge) > topk)
    def _():
        need = topk - count(lambda blk, k0: blk > tau)
        pos = lax.broadcasted_iota(jnp.int32, (cr, tq), 0)
        pos_bits = max(1, (seq - 1).bit_length())

        def pos_iter(it, last):
            cand = last + jnp.left_shift(jnp.int32(1), pos_bits - 1 - it)
            below = count(lambda blk, k0: jnp.where(blk == tau, pos + k0, cand) < cand)
            return jnp.where(below < need, cand, last)

        last = lax.fori_loop(0, pos_bits, pos_iter, jnp.zeros((1, tq), jnp.int32))

        def demote(j, carry):
            k0 = pl.multiple_of(j * cr, cr)
            blk = keys_ref[pl.ds(k0, cr), :]
            drop = jnp.where(blk == tau, pos + k0, last) > last
            keys_ref[pl.ds(k0, cr), :] = jnp.where(drop, tau - 1, blk)
            return carry

        lax.fori_loop(0, n_ct, demote, 0)

    m_ref[...] = jnp.full(m_ref.shape, NEG_BIG, F32)
    acc_ref[...] = jnp.zeros(acc_ref.shape, F32)

    eye = (lax.broadcasted_iota(jnp.int32, (tq, tq), 0) == lax.broadcasted_iota(jnp.int32, (tq, tq), 1))
    eye = jnp.concatenate([jnp.where(eye, 1.0, 0.0).astype(BF16)] * grp, axis=1)
    for g in range(n_kv):
        qtg = jnp.concatenate([qt_ref[0, (g * grp + i) * HEAD_DIM:(g * grp + i + 1) * HEAD_DIM, :]
                               for i in range(grp)], axis=1)
        qb_ref[g] = jnp.concatenate([qtg, eye], axis=0)
    ones = jnp.ones((ONES_ROWS, ar), BF16)

    def logits_stage(j, slot):
        k0 = pl.multiple_of(jnp.minimum(j, n_at - 1) * ar, ar)
        tau_j = jnp.where(j < n_at, tau, jnp.iinfo(jnp.int32).max)
        bias = jnp.where(keys_ref[pl.ds(k0, ar), :] >= tau_j, 0.0, NEG_BIG).astype(BF16)
        for g in range(n_kv):
            kg = k_ref[0, pl.ds(k0, ar), g * HEAD_DIM:(g + 1) * HEAD_DIM]
            s = jnp.dot(jnp.concatenate([kg, bias], axis=1), qb_ref[g], preferred_element_type=F32)
            s_ref[slot, g] = s
            smax_ref[slot, g] = jnp.max(s, axis=0, keepdims=True)

    def softmax_stage(j, slot):
        k0 = pl.multiple_of(jnp.minimum(j, n_at - 1) * ar, ar)
        for g in range(n_kv):
            m_old = m_ref[g]
            m_new = jnp.maximum(m_old, smax_ref[slot, g])
            alpha = jnp.exp2(m_old - m_new)
            p = jnp.exp2(s_ref[slot, g] - m_new).astype(BF16)
            vtg = jnp.concatenate([vt_ref[0, g * HEAD_DIM:(g + 1) * HEAD_DIM, pl.ds(k0, ar)], ones], axis=0)
            acc_ref[g] = alpha * acc_ref[g] + jnp.dot(vtg, p, preferred_element_type=F32)
            m_ref[g] = m_new

    logits_stage(0, 0)

    def att_pair(i, carry):
        logits_stage(2 * i + 1, 1)
        softmax_stage(2 * i, 0)
        logits_stage(2 * i + 2, 0)
        softmax_stage(2 * i + 1, 1)
        return carry

    lax.fori_loop(0, (n_at + 1) // 2, att_pair, 0)

    for g in range(n_kv):
        o = acc_ref[g, :HEAD_DIM, :] / acc_ref[g, HEAD_DIM:HEAD_DIM + 1, :]
        for i in range(grp):
            h = g * grp + i
            o_ref[0, :, h * HEAD_DIM:(h + 1) * HEAD_DIM] = o[:, i * tq:(i + 1) * tq].T.astype(o_ref.dtype)


def _dsa(qt, qit, wt, k2, k, vt, topk):
    bsz, aw, seq = qt.shape
    tq = LANES
    n_idx = wt.shape[1]
    kw = k.shape[2]
    n_kv = kw // HEAD_DIM
    grp = aw // kw
    once = pl.Buffered(1)
    return pl.pallas_call(
        functools.partial(_dsa_body, topk=topk),
        grid=(bsz, seq // tq),
        in_specs=[
            pl.BlockSpec((1, aw, tq), lambda b, i: (b, 0, i)),
            pl.BlockSpec((1, qit.shape[1], tq), lambda b, i: (b, 0, i)),
            pl.BlockSpec((1, n_idx, tq), lambda b, i: (b, 0, i)),
            pl.BlockSpec((1, seq, 2 * IDX_DIM), lambda b, i: (b, 0, 0), pipeline_mode=once),
            pl.BlockSpec((1, seq, kw), lambda b, i: (b, 0, 0), pipeline_mode=once),
            pl.BlockSpec((1, kw, seq), lambda b, i: (b, 0, 0), pipeline_mode=once),
        ],
        out_specs=pl.BlockSpec((1, tq, aw), lambda b, i: (b, i, 0)),
        out_shape=jax.ShapeDtypeStruct((bsz, seq, aw), BF16),
        scratch_shapes=[
            pltpu.VMEM((seq, tq), jnp.int32),
            pltpu.VMEM((-(-topk // DSA_IDX_ROWS) * DSA_IDX_ROWS, tq), jnp.int32),
            pltpu.VMEM((n_idx // 2, 2 * IDX_DIM, 2 * tq), BF16),
            pltpu.VMEM((n_kv, HEAD_DIM + tq, grp * tq), BF16),
            pltpu.VMEM((n_kv, 1, grp * tq), F32),
            pltpu.VMEM((n_kv, HEAD_DIM + ONES_ROWS, grp * tq), F32),
            pltpu.VMEM((2, n_kv, DSA_ATT_ROWS, grp * tq), F32),
            pltpu.VMEM((2, n_kv, 1, grp * tq), F32),
        ],
        compiler_params=_params(("parallel", "arbitrary")),
        name="dsa",
    )(qt, qit, wt, k2, k, vt)


def _conv_silu(blk_ref, halo_ref, buf_ref, w_ref, b_ref, ts):
    blk = blk_ref[0]
    buf_ref[0:8, :] = halo_ref[...]
    buf_ref[8:8 + ts, :] = blk
    halo_ref[...] = blk[ts - 8:ts, :]
    out = b_ref[...] + w_ref[CONV_WIDTH - 1:CONV_WIDTH, :] * blk
    for k in range(CONV_WIDTH - 1):
        out = out + w_ref[k:k + 1, :] * buf_ref[pl.ds(8 - (CONV_WIDTH - 1) + k, ts), :]
    return _silu(out)


def _ssd_body(z_ref, x_ref, b_ref, c_ref, dt_ref, dtt_ref, alog_row_ref, alog_col_ref, dskip_ref, gn_ref,
              wx_ref, wb_ref, wc_ref, bx_ref, bb_ref, bc_ref, o_ref,
              ht_ref, hx_ref, hb_ref, hc_ref, bufx_ref, bufb_ref, bufc_ref):
    ts = x_ref.shape[1]
    hg = dt_ref.shape[3]
    hd = SSM_HEAD_DIM
    ll = SSD_L

    @pl.when(pl.program_id(2) == 0)
    def _():
        ht_ref[...] = jnp.zeros(ht_ref.shape, F32)
        hx_ref[...] = jnp.zeros(hx_ref.shape, F32)
        hb_ref[...] = jnp.zeros(hb_ref.shape, F32)
        hc_ref[...] = jnp.zeros(hc_ref.shape, F32)

    xs = _conv_silu(x_ref, hx_ref, bufx_ref, wx_ref, bx_ref, ts)
    bm = _conv_silu(b_ref, hb_ref, bufb_ref, wb_ref, bb_ref, ts)
    cm = _conv_silu(c_ref, hc_ref, bufc_ref, wc_ref, bc_ref, ts)

    a_row = -jnp.exp(alog_row_ref[0])
    a_col = -jnp.exp(alog_col_ref[0])
    ti = lax.broadcasted_iota(jnp.int32, (ll, ll), 0)
    si = lax.broadcasted_iota(jnp.int32, (ll, ll), 1)
    causal = si <= ti
    tril = jnp.where(causal, 1.0, 0.0).astype(F32)
    triu = jnp.where(ti <= si, 1.0, 0.0).astype(F32)
    lane = lax.broadcasted_iota(jnp.int32, (1, 2 * hd), 1)
    first = lane < hd

    for ci in range(ts // ll):
        rows = slice(ci * ll, (ci + 1) * ll)
        dt = dt_ref[0, 0, rows, :]
        dtt = dtt_ref[0, 0, :, rows]
        acol = jnp.dot(tril, dt * a_row, preferred_element_type=F32, precision=lax.Precision.HIGHEST)
        arow = jnp.dot(dtt * a_col, triu, preferred_element_type=F32, precision=lax.Precision.HIGHEST)
        alast = arow[:, ll - 1:ll]
        wrow = dtt * jnp.exp(alast - arow)
        elast = jnp.exp(alast)
        xc = xs[rows]
        bc = bm[rows]
        cc = cm[rows]
        bcb = bc.astype(BF16)
        ccb = cc.astype(BF16)
        cb = lax.dot_general(ccb, bcb, (((1,), (1,)), ((), ())), preferred_element_type=F32)
        bt = bc.T
        ys = []
        for p in range(hg // 2):
            xp = xc[:, p * 2 * hd:(p + 1) * 2 * hd].astype(BF16)
            hp = ht_ref[p]
            rhs = jnp.concatenate([xp, hp.astype(BF16)], axis=0)
            y2, s2, e2 = [], [], []
            for e in range(2):
                j = 2 * p + e
                bcol = jnp.broadcast_to(acol[:, j:j + 1], (ll, ll))
                dec = jnp.where(causal, jnp.exp(bcol - arow[j:j + 1, :]), 0.0)
                mm = cb * dec * dtt[j:j + 1, :]
                ecol = jnp.exp(jnp.broadcast_to(acol[:, j:j + 1], (ll, SSM_STATE)))
                lhs = jnp.concatenate([mm, cc * ecol], axis=1).astype(BF16)
                y2.append(jnp.dot(lhs, rhs, preferred_element_type=F32))
                s2.append(jnp.dot((bt * wrow[j:j + 1, :]).astype(BF16), xp, preferred_element_type=F32))
                e2.append(jnp.broadcast_to(elast[j:j + 1, :], (1, 2 * hd)))
            ys.append(jnp.where(first, y2[0], y2[1]))
            ht_ref[p] = hp * jnp.where(first, e2[0], e2[1]) + jnp.where(first, s2[0], s2[1])
        y = jnp.concatenate(ys, axis=1)
        y = (y + dskip_ref[0] * xc) * _silu(z_ref[0, rows, :])
        y = y * lax.rsqrt(jnp.mean(y * y, axis=-1, keepdims=True) + NORM_EPS)
        o_ref[0, rows, :] = (y * gn_ref[0]).astype(o_ref.dtype)


def _ssd(zx, dt, conv_w, conv_b, a_log, d_skip, g_norm):
    bsz, seq, _ = zx.shape
    heads = dt.shape[2]
    g = SSM_GROUPS
    hg = heads // g
    wid = heads * SSM_HEAD_DIM
    gw = wid // g
    n = SSM_STATE
    ts = SSD_TS
    dtg = dt.reshape(bsz, seq, g, hg).transpose(0, 2, 1, 3)
    dtt = dtg.transpose(0, 1, 3, 2)
    zb, xb, bb, cb = 0, wid // gw, 2 * wid // n, (2 * wid + g * n) // n
    cw_x, cw_b, cw_c = conv_w[:, :wid], conv_w[:, wid:wid + g * n], conv_w[:, wid + g * n:]
    cb_x, cb_b, cb_c = (conv_b[None, :wid], conv_b[None, wid:wid + g * n], conv_b[None, wid + g * n:])
    col = lambda off: (lambda b, gi, c: (b, c, off + gi))
    par = lambda b, gi, c: (0, gi)
    return pl.pallas_call(
        _ssd_body,
        grid=(bsz, g, seq // ts),
        in_specs=[
            pl.BlockSpec((1, ts, gw), col(zb)),
            pl.BlockSpec((1, ts, gw), col(xb)),
            pl.BlockSpec((1, ts, n), col(bb)),
            pl.BlockSpec((1, ts, n), col(cb)),
            pl.BlockSpec((1, 1, ts, hg), lambda b, gi, c: (b, gi, c, 0)),
            pl.BlockSpec((1, 1, hg, ts), lambda b, gi, c: (b, gi, 0, c)),
            pl.BlockSpec((1, 1, hg), lambda b, gi, c: (gi, 0, 0)),
            pl.BlockSpec((1, hg, 1), lambda b, gi, c: (gi, 0, 0)),
            pl.BlockSpec((1, 1, gw), lambda b, gi, c: (gi, 0, 0)),
            pl.BlockSpec((1, 1, gw), lambda b, gi, c: (gi, 0, 0)),
            pl.BlockSpec((CONV_WIDTH, gw), par),
            pl.BlockSpec((CONV_WIDTH, n), par),
            pl.BlockSpec((CONV_WIDTH, n), par),
            pl.BlockSpec((1, gw), par),
            pl.BlockSpec((1, n), par),
            pl.BlockSpec((1, n), par),
        ],
        out_specs=pl.BlockSpec((1, ts, gw), lambda b, gi, c: (b, c, gi)),
        out_shape=jax.ShapeDtypeStruct((bsz, seq, wid), BF16),
        scratch_shapes=[
            pltpu.VMEM((hg // 2, n, 2 * SSM_HEAD_DIM), F32),
            pltpu.VMEM((8, gw), F32), pltpu.VMEM((8, n), F32), pltpu.VMEM((8, n), F32),
            pltpu.VMEM((ts + 8, gw), F32), pltpu.VMEM((ts + 8, n), F32), pltpu.VMEM((ts + 8, n), F32),
        ],
        compiler_params=_params(("parallel", "parallel", "arbitrary")),
        name="ssd",
    )(zx, zx, zx, zx, dtg, dtt, a_log.reshape(g, 1, hg), a_log.reshape(g, hg, 1),
      jnp.repeat(d_skip, SSM_HEAD_DIM).reshape(g, 1, gw), g_norm.reshape(g, 1, gw),
      cw_x, cw_b, cw_c, cb_x, cb_b, cb_c)


def _merge_body(u_ref, ya_ref, ys_ref, wg0_ref, wg1_ref, wa_ref, ws_ref, o_ref):
    u = u_ref[...]
    g0 = jax.nn.sigmoid(jnp.dot(u, wg0_ref[...], preferred_element_type=F32))
    g1 = jax.nn.sigmoid(jnp.dot(u, wg1_ref[...], preferred_element_type=F32))
    a = jnp.dot(ya_ref[...], wa_ref[...], preferred_element_type=F32)
    s = jnp.dot(ys_ref[...], ws_ref[...], preferred_element_type=F32)
    o_ref[...] = (g0 * a + g1 * s).astype(o_ref.dtype)


def _merge(u, ya, ys, w_gate, w_a, w_s):
    m, d = u.shape
    tm, tn = MERGE_TM, MERGE_TN
    nj = d // tn
    return pl.pallas_call(
        _merge_body,
        grid=(m // tm, nj),
        in_specs=[
            pl.BlockSpec((tm, d), lambda i, j: (i, 0)),
            pl.BlockSpec((tm, ya.shape[1]), lambda i, j: (i, 0)),
            pl.BlockSpec((tm, ys.shape[1]), lambda i, j: (i, 0)),
            pl.BlockSpec((d, tn), lambda i, j: (0, j)),
            pl.BlockSpec((d, tn), lambda i, j: (0, nj + j)),
            pl.BlockSpec((w_a.shape[0], tn), lambda i, j: (0, j)),
            pl.BlockSpec((w_s.shape[0], tn), lambda i, j: (0, j)),
        ],
        out_specs=pl.BlockSpec((tm, tn), lambda i, j: (i, j)),
        out_shape=jax.ShapeDtypeStruct((m, d), BF16),
        compiler_params=_params(("parallel", "arbitrary")),
        name="merge",
    )(u, ya, ys, w_gate, w_gate, w_a, w_s)


def _out_body(mg_ref, x_ref, w_ref, g_ref, h_ref, u_ref):
    h = x_ref[...] + jnp.dot(mg_ref[...], w_ref[...], preferred_element_type=F32)
    h_ref[...] = h
    y = h * lax.rsqrt(jnp.mean(h * h, axis=-1, keepdims=True) + NORM_EPS)
    u_ref[...] = (y * g_ref[...]).astype(u_ref.dtype)


def _out_proj(merged, x2, w_out, g_ffn):
    m, d = x2.shape
    tm = OUT_TM
    row = pl.BlockSpec((tm, d), lambda i: (i, 0))
    return pl.pallas_call(
        _out_body,
        grid=(m // tm,),
        in_specs=[row, row, pl.BlockSpec((d, d), lambda i: (0, 0)), pl.BlockSpec((1, d), lambda i: (0, 0))],
        out_specs=[row, row],
        out_shape=[jax.ShapeDtypeStruct((m, d), F32), jax.ShapeDtypeStruct((m, d), BF16)],
        compiler_params=_params(("parallel",)),
        name="out_proj",
    )(merged, x2, w_out, g_ffn.reshape(1, d))


def _ffn_up_body(u_ref, wg_ref, wu_ref, o_ref):
    u = u_ref[...]
    gate = jnp.dot(u, wg_ref[...], preferred_element_type=F32)
    up = jnp.dot(u, wu_ref[...], preferred_element_type=F32)
    o_ref[...] = (_silu(gate) * up).astype(o_ref.dtype)


def _ffn_up(u2, w_in):
    m, d = u2.shape
    dff = w_in.shape[1] // 2
    tm, tn = FFN_TM, FFN_TN
    nj = dff // tn
    return pl.pallas_call(
        _ffn_up_body,
        grid=(m // tm, nj),
        in_specs=[pl.BlockSpec((tm, d), lambda i, j: (i, 0)),
                  pl.BlockSpec((d, tn), lambda i, j: (0, j)),
                  pl.BlockSpec((d, tn), lambda i, j: (0, nj + j))],
        out_specs=pl.BlockSpec((tm, tn), lambda i, j: (i, j)),
        out_shape=jax.ShapeDtypeStruct((m, dff), BF16),
        compiler_params=_params(("parallel", "arbitrary")),
        name="ffn_up",
    )(u2, w_in, w_in)


def _ffn_down_body(a_ref, w_ref, h_ref, g_ref, o_ref):
    h = h_ref[...] + jnp.dot(a_ref[...], w_ref[...], preferred_element_type=F32)
    y = h * lax.rsqrt(jnp.mean(h * h, axis=-1, keepdims=True) + NORM_EPS)
    o_ref[...] = y * g_ref[...]


def _ffn_down(act, w_out, h1, g_final):
    m, dff = act.shape
    d = w_out.shape[1]
    tm = DOWN_TM
    return pl.pallas_call(
        _ffn_down_body,
        grid=(m // tm,),
        in_specs=[pl.BlockSpec((tm, dff), lambda i: (i, 0)),
                  pl.BlockSpec((dff, d), lambda i: (0, 0), pipeline_mode=pl.Buffered(1)),
                  pl.BlockSpec((tm, d), lambda i: (i, 0)),
                  pl.BlockSpec((1, d), lambda i: (0, 0))],
        out_specs=pl.BlockSpec((tm, d), lambda i: (i, 0)),
        out_shape=jax.ShapeDtypeStruct((m, d), F32),
        compiler_params=_params(("parallel",)),
        name="ffn_down",
    )(act, w_out, h1, g_final.reshape(1, d))


def _layer(h, w_in, w_gate, w_attn_branch, w_ssm_branch, w_out, conv_w, conv_b, dt_bias, a_log, d_skip,
           g_ssm_norm, g_mix, g_ffn, w_ffn_in, w_ffn_out, g_last):
    bsz, seq, d = h.shape
    m = bsz * seq
    aw = N_HEADS * HEAD_DIM
    kw = N_KV_HEADS * HEAD_DIM
    iw = N_IDX_HEADS * IDX_DIM
    sw = SSM_EXPAND * d
    heads = sw // SSM_HEAD_DIM
    bcw = SSM_GROUPS * SSM_STATE
    cuts = [0]
    for s in (aw, kw, kw, iw, IDX_DIM, N_IDX_HEADS, sw, sw + 2 * bcw, heads):
        cuts.append(cuts[-1] + s)
    wb = w_in.astype(BF16)
    seg = lambda i, j=None: wb[:, cuts[i]:cuts[i + 1 if j is None else j]]
    pad = lambda a, n: jnp.pad(a, ((0, 0), (0, n - a.shape[1])))
    w_small = jnp.concatenate([seg(4), seg(4), pad(seg(5), LANES), pad(seg(8), LANES)], axis=1)

    x2 = h.reshape(m, d)
    u = _rmsnorm(x2, g_mix)
    u3 = u.reshape(bsz, seq, d)

    tm = min(PROJ_TM, seq)
    cos_a, sin_a = _rope_tables(seq, HEAD_DIM)
    cos_i, sin_i = _rope_tables(seq, IDX_DIM)
    tab = _table_spec(tm)
    blk = lambda n: pl.BlockSpec((1, tm, n), lambda b, i, j: (b, i, j))
    blk_t = lambda n: pl.BlockSpec((1, n, tm), lambda b, i, j: (b, j, i))

    tn = min(PROJ_TN, aw)
    qt = _proj_call(u3, seg(0), functools.partial(_rope_t_epilogue, HEAD_DIM, HEAD_DIM ** -0.5 * math.log2(math.e)),
                    (cos_a, sin_a), (tab, tab), jax.ShapeDtypeStruct((bsz, aw, seq), BF16), blk_t(tn), tm, tn,
                    "proj_q")
    k, vt = _proj_call(u3, seg(1, 3), functools.partial(_kv_epilogue, kw), (cos_a, sin_a), (tab, tab),
                       [jax.ShapeDtypeStruct((bsz, seq, kw), BF16), jax.ShapeDtypeStruct((bsz, kw, seq), BF16)],
                       [pl.BlockSpec((1, tm, kw), lambda b, i, j: (b, i, 0)),
                        pl.BlockSpec((1, kw, tm), lambda b, i, j: (b, 0, i))], tm, 2 * kw, "proj_kv")
    tn = min(PROJ_TN, iw)
    qit = _proj_call(u3, seg(3), functools.partial(_rope_t_epilogue, IDX_DIM, 1.0),
                     (cos_i, sin_i), (tab, tab), jax.ShapeDtypeStruct((bsz, iw, seq), BF16), blk_t(tn), tm, tn,
                     "proj_qidx")
    k2, wt, dt = _proj_call(
        u3, w_small, functools.partial(_small_epilogue, N_IDX_HEADS, heads),
        (cos_i, sin_i, dt_bias.reshape(1, heads)),
        (tab, tab, pl.BlockSpec((1, heads), lambda b, i, j: (0, 0))),
        [jax.ShapeDtypeStruct((bsz, seq, 2 * IDX_DIM), BF16), jax.ShapeDtypeStruct((bsz, N_IDX_HEADS, seq), F32),
         jax.ShapeDtypeStruct((bsz, seq, heads), F32)],
        [pl.BlockSpec((1, tm, 2 * IDX_DIM), lambda b, i, j: (b, i, 0)),
         pl.BlockSpec((1, N_IDX_HEADS, tm), lambda b, i, j: (b, 0, i)),
         pl.BlockSpec((1, tm, heads), lambda b, i, j: (b, i, 0))], tm, 3 * LANES, "proj_small")
    nzx = 2 * sw + 2 * bcw
    tn = min(PROJ_TN, nzx)
    zx = _proj_call(u3, seg(6, 8), _plain_epilogue, (), (), jax.ShapeDtypeStruct((bsz, seq, nzx), F32),
                    blk(tn), tm, tn, "proj_zx")

    y_attn = _dsa(qt, qit, wt, k2, k, vt, min(TOPK_MAX, seq // 4))
    y_ssm = _ssd(zx, dt, conv_w, conv_b, a_log, d_skip, g_ssm_norm)

    merged = _merge(u, y_attn.reshape(m, aw), y_ssm.reshape(m, sw), w_gate.astype(BF16),
                    w_attn_branch.astype(BF16), w_ssm_branch.astype(BF16))
    h1, u2 = _out_proj(merged, x2, w_out.astype(BF16), g_ffn)
    act = _ffn_up(u2, w_ffn_in.astype(BF16))
    return _ffn_down(act, w_ffn_out.astype(BF16), h1, g_last).reshape(bsz, seq, d)


def kernel(x, w_in, w_gate, w_attn_branch, w_ssm_branch, w_out, conv_w, conv_b, dt_bias, a_log, d_skip, g_ssm_norm,
           g_mix, g_ffn, w_ffn_in, w_ffn_out, g_final):
    depth = w_in.shape[0]
    assert depth == 1, "the final norm is fused into the last layer's down projection"
    return _layer(x, w_in[0], w_gate[0], w_attn_branch[0], w_ssm_branch[0], w_out[0], conv_w[0], conv_b[0],
                  dt_bias[0], a_log[0], d_skip[0], g_ssm_norm[0], g_mix[0], g_ffn[0], w_ffn_in[0], w_ffn_out[0],
                  g_final)
```

```python
import functools
import math

import jax
import jax.numpy as jnp
from jax import lax
from jax.experimental import pallas as pl
from jax.experimental.pallas import tpu as pltpu

CHUNK = 64
ROPE_THETA = 10000.0
NORM_EPS = 1e-6
N_HEADS = 16
N_KV_HEADS = 4
HEAD_DIM = 128
TOPK_MAX = 256
N_IDX_HEADS = 16
IDX_DIM = 64
SSM_EXPAND = 2
SSM_HEAD_DIM = 64
SSM_GROUPS = 8
SSM_STATE = 128
CONV_WIDTH = 4

LANES = 128
VMEM_LIMIT = 60 * 1024 * 1024

PROJ_TM = 1024
PROJ_TN = 1024
DSA_IDX_ROWS = 128
DSA_STEP_ROWS = 512
DSA_CNT_ROWS = 1024
DSA_ATT_ROWS = 256
FUSED_CNT_TILES = 8
CNT_ACC_ROWS = 32
ONES_ROWS = 16
BISECT_STEPS_PER_ROUND = 2
BISECT_MAX_ROUNDS = 17
SSD_TS = 256
SSD_L = 128
MERGE_TM = 512
MERGE_TN = 512
OUT_TM = 512
FFN_TM = 1024
FFN_TN = 512
DOWN_TM = 512

F32 = jnp.float32
BF16 = jnp.bfloat16
INT_MIN = -2147483648
NEG_BIG = -1e30


def _params(sem):
    return pltpu.CompilerParams(dimension_semantics=sem, vmem_limit_bytes=VMEM_LIMIT)


def _silu(x):
    h = 0.5 * x
    return h + h * jnp.tanh(h)


def _rms_body(x_ref, g_ref, o_ref):
    x = x_ref[...]
    y = x * lax.rsqrt(jnp.mean(x * x, axis=-1, keepdims=True) + NORM_EPS)
    o_ref[...] = (y * g_ref[...]).astype(o_ref.dtype)


def _rmsnorm(x2, g, tm=512):
    m, d = x2.shape
    return pl.pallas_call(
        _rms_body,
        grid=(m // tm,),
        in_specs=[pl.BlockSpec((tm, d), lambda i: (i, 0)), pl.BlockSpec((1, d), lambda i: (0, 0))],
        out_specs=pl.BlockSpec((tm, d), lambda i: (i, 0)),
        out_shape=jax.ShapeDtypeStruct((m, d), BF16),
        compiler_params=_params(("parallel",)),
        name="rmsnorm",
    )(x2, g.reshape(1, d))


def _rope_tables(seq, dim):
    pos = jnp.arange(seq, dtype=F32)
    inv = ROPE_THETA ** (-jnp.arange(0, dim, 2, dtype=F32) / dim)
    ang = pos[:, None] * inv[None, :]
    c, s = jnp.cos(ang), jnp.sin(ang)
    reps = LANES // dim
    return (jnp.tile(jnp.concatenate([c, c], axis=-1), (1, reps)),
            jnp.tile(jnp.concatenate([-s, s], axis=-1), (1, reps)))


def _swap_halves(x, dim):
    if dim == LANES:
        return pltpu.roll(x, LANES // 2, axis=1)
    lane = lax.broadcasted_iota(jnp.int32, x.shape, 1)
    first = (lane % dim) < (dim // 2)
    return jnp.where(first, pltpu.roll(x, LANES - dim // 2, axis=1), pltpu.roll(x, dim // 2, axis=1))


def _rope(x, cos_t, sin_t, dim):
    return x * cos_t + _swap_halves(x, dim) * sin_t


def _proj_call(u, w, epilogue, extras, extra_specs, out_shape, out_specs, tm, tn, name):
    bsz, seq, k = u.shape
    n = w.shape[1]

    def body(u_ref, w_ref, *refs):
        acc = jnp.dot(u_ref[0], w_ref[...], preferred_element_type=F32)
        epilogue(acc, *refs)

    return pl.pallas_call(
        body,
        grid=(bsz, seq // tm, n // tn),
        in_specs=[pl.BlockSpec((1, tm, k), lambda b, i, j: (b, i, 0)),
                  pl.BlockSpec((k, tn), lambda b, i, j: (0, j))] + list(extra_specs),
        out_specs=out_specs,
        out_shape=out_shape,
        compiler_params=_params(("parallel", "parallel", "arbitrary")),
        name=name,
    )(u, w, *extras)


def _table_spec(tm):
    return pl.BlockSpec((tm, LANES), lambda b, i, j: (i, 0))


def _rope_t_epilogue(dim, scale, acc, cos_ref, sin_ref, o_ref):
    c, s = cos_ref[...], sin_ref[...]
    for t in range(acc.shape[1] // LANES):
        sl = slice(t * LANES, (t + 1) * LANES)
        o_ref[0, sl, :] = (_rope(acc[:, sl], c, s, dim) * scale).T.astype(o_ref.dtype)


def _kv_epilogue(kw, acc, cos_ref, sin_ref, k_ref, vt_ref):
    c, s = cos_ref[...], sin_ref[...]
    for t in range(kw // LANES):
        sl = slice(t * LANES, (t + 1) * LANES)
        k_ref[0, :, sl] = _rope(acc[:, sl], c, s, HEAD_DIM).astype(k_ref.dtype)
        vt_ref[0, sl, :] = acc[:, kw + t * LANES:kw + (t + 1) * LANES].T.astype(vt_ref.dtype)


def _small_epilogue(n_idx, n_dt, acc, cos_ref, sin_ref, bias_ref, k2_ref, wt_ref, dt_ref):
    k2_ref[0] = _rope(acc[:, :LANES], cos_ref[...], sin_ref[...], IDX_DIM).astype(k2_ref.dtype)
    wt_ref[0] = acc[:, LANES:2 * LANES].T[:n_idx]
    x = acc[:, 2 * LANES:2 * LANES + n_dt] + bias_ref[...]
    dt_ref[0] = jnp.maximum(x, 0.0) + jnp.log1p(jnp.exp(-jnp.abs(x)))


def _plain_epilogue(acc, o_ref):
    o_ref[0] = acc.astype(o_ref.dtype)


def _to_key(x):
    bits = pltpu.bitcast(x, jnp.int32)
    return jnp.where(bits < 0, bits ^ jnp.int32(0x7FFFFFFF), bits)


def _dsa_body(qt_ref, qit_ref, wt_ref, k2_ref, k_ref, vt_ref, o_ref,
              keys_ref, slot_ref, bias_ref, rhs_ref, qb_ref, m_ref, acc_ref, s_ref, smax_ref, *, topk):
    tq = qt_ref.shape[2]
    n_idx = wt_ref.shape[1]
    seq = k_ref.shape[1]
    n_kv = k_ref.shape[2] // HEAD_DIM
    grp = qt_ref.shape[1] // HEAD_DIM // n_kv
    ir, ar = DSA_IDX_ROWS, DSA_ATT_ROWS
    cr = min(DSA_CNT_ROWS, seq)
    st = min(DSA_STEP_ROWS, cr)
    step = pl.program_id(1)
    n_blocks = pl.num_programs(1) - 1
    has_cur = step < n_blocks
    has_prev = step >= 1
    q0 = jnp.minimum(step, n_blocks - 1) * tq
    n_ct = jnp.where(has_cur, (q0 + tq + cr - 1) // cr, 0)
    n_at = jnp.where(has_prev, (step * tq + ar - 1) // ar, 0)

    zero = jnp.zeros((IDX_DIM, tq), rhs_ref.dtype)
    for p in range(n_idx // 2):
        top = qit_ref[0, 2 * p * IDX_DIM:(2 * p + 1) * IDX_DIM, :]
        bot = qit_ref[0, (2 * p + 1) * IDX_DIM:(2 * p + 2) * IDX_DIM, :]
        rhs_ref[p] = jnp.concatenate([jnp.concatenate([top, zero], axis=1),
                                      jnp.concatenate([zero, bot], axis=1)], axis=0)

    qpos = q0 + lax.broadcasted_iota(jnp.int32, (ir, tq), 1)
    limit = (qpos // CHUNK + 1) * CHUNK
    krow = lax.broadcasted_iota(jnp.int32, (ir, tq), 0)

    def idx_tile(j, carry):
        for sub in range(st // ir):
            k0 = pl.multiple_of(j * st + sub * ir, ir)
            kblk = k2_ref[0, pl.ds(k0, ir), :]
            acc = jnp.zeros((ir, tq), F32)
            for p in range(n_idx // 2):
                d = jnp.dot(kblk, rhs_ref[p], preferred_element_type=F32)
                acc = acc + jnp.maximum(d[:, :tq], 0.0) * wt_ref[0, 2 * p:2 * p + 1, :]
                acc = acc + jnp.maximum(d[:, tq:], 0.0) * wt_ref[0, 2 * p + 1:2 * p + 2, :]
            key = jnp.where(krow + k0 < limit, _to_key(acc), INT_MIN)
            keys_ref[pl.ds(k0, ir), :] = key
            s0 = pl.multiple_of(lax.rem(k0, slots), ir)
            slot_ref[pl.ds(s0, ir), :] = jnp.maximum(slot_ref[pl.ds(s0, ir), :], key)
        return carry

    slots = slot_ref.shape[0]
    slot_ref[...] = jnp.full(slot_ref.shape, INT_MIN, jnp.int32)
    lax.fori_loop(0, n_ct * (cr // st), idx_tile, 0)

    def count(pred):
        def cnt_tile(j, cnt):
            k0 = pl.multiple_of(j * cr, cr)
            hit = jnp.where(pred(keys_ref[pl.ds(k0, cr), :], k0), 1, 0)
            return cnt + jnp.sum(hit.reshape(cr // CNT_ACC_ROWS, CNT_ACC_ROWS, tq), axis=0)

        cnt = lax.fori_loop(0, n_ct, cnt_tile, jnp.zeros((CNT_ACC_ROWS, tq), jnp.int32))
        return jnp.sum(cnt, axis=0, keepdims=True)

    slot = slot_ref[...]
    lo = jnp.min(slot, axis=0, keepdims=True)
    hi = jnp.max(slot, axis=0, keepdims=True) + 1
    few = limit[:1, :] <= topk

    def halve(lo, hi, n_lo, mid, total, done):
        up = done & (total >= topk)
        dn = done & (total < topk)
        return jnp.where(up, mid, lo), jnp.where(dn, mid, hi), jnp.where(up, total, n_lo)

    def mean(lo, hi):
        return (lo >> 1) + (hi >> 1) + (lo & hi & 1)

    ones = jnp.ones((ONES_ROWS, ar), BF16)

    def logits_stage(j, slot):
        k0 = pl.multiple_of(jnp.clip(j, 0, n_at - 1) * ar, ar)
        bias = jnp.where(j < n_at, bias_ref[pl.ds(k0, ar), :], jnp.asarray(NEG_BIG, BF16))
        for g in range(n_kv):
            kg = k_ref[0, pl.ds(k0, ar), g * HEAD_DIM:(g + 1) * HEAD_DIM]
            s = jnp.dot(jnp.concatenate([kg, bias], axis=1), qb_ref[g], preferred_element_type=F32)
            s_ref[slot, g] = s
            smax_ref[slot, g] = jnp.max(s, axis=0, keepdims=True)

    def softmax_stage(j, slot):
        k0 = pl.multiple_of(jnp.clip(j, 0, n_at - 1) * ar, ar)
        for g in range(n_kv):
            m_old = m_ref[g]
            m_new = jnp.maximum(m_old, smax_ref[slot, g])
            alpha = jnp.exp2(m_old - m_new)
            p = jnp.exp2(s_ref[slot, g] - m_new).astype(BF16)
            vtg = jnp.concatenate([vt_ref[0, g * HEAD_DIM:(g + 1) * HEAD_DIM, pl.ds(k0, ar)], ones], axis=0)
            acc_ref[g] = alpha * acc_ref[g] + jnp.dot(vtg, p, preferred_element_type=F32)
            m_ref[g] = m_new

    @pl.when(has_prev)
    def _():
        m_ref[...] = jnp.full(m_ref.shape, NEG_BIG, F32)
        acc_ref[...] = jnp.zeros(acc_ref.shape, F32)
        logits_stage(0, 0)

    def fused_step(i, carry):
        lo, hi, n_lo, cnt, base = carry
        mid = mean(lo, hi)
        logits_stage(2 * i + 1, 1)
        for u in range(FUSED_CNT_TILES):
            t = base + u
            k0 = pl.multiple_of(jnp.clip(t, 0, jnp.maximum(n_ct - 1, 0)) * cr, cr)
            mid_u = jnp.where(t < n_ct, mid, jnp.iinfo(jnp.int32).max)
            hit = jnp.where(keys_ref[pl.ds(k0, cr), :] >= mid_u, 1, 0)
            cnt = cnt + jnp.sum(hit.reshape(cr // CNT_ACC_ROWS, CNT_ACC_ROWS, tq), axis=0)
        softmax_stage(2 * i, 0)
        logits_stage(2 * i + 2, 0)
        softmax_stage(2 * i + 1, 1)
        done = base + FUSED_CNT_TILES >= n_ct
        lo, hi, n_lo = halve(lo, hi, n_lo, mid, jnp.sum(cnt, axis=0, keepdims=True), done)
        return (lo, hi, n_lo, jnp.where(done, 0, cnt), jnp.where(done, 0, base + FUSED_CNT_TILES))

    n_lo = jnp.full((1, tq), topk + 1, jnp.int32)
    lo, hi, n_lo, _, _ = lax.fori_loop(0, (n_at + 1) // 2, fused_step,
                                       (lo, hi, n_lo, jnp.zeros((CNT_ACC_ROWS, tq), jnp.int32), jnp.int32(0)))

    @pl.when(has_prev)
    def _():
        for g in range(n_kv):
            o = acc_ref[g, :HEAD_DIM, :] / acc_ref[g, HEAD_DIM:HEAD_DIM + 1, :]
            for i in range(grp):
                h = g * grp + i
                o_ref[0, :, h * HEAD_DIM:(h + 1) * HEAD_DIM] = o[:, i * tq:(i + 1) * tq].T.astype(o_ref.dtype)

    @pl.when(has_cur)
    def _():
        def settled(lo, hi, n_lo):
            ok = few | (n_lo == topk) | (hi - 1 <= lo)
            return jnp.min(jnp.where(ok, 1, 0))

        def bis_cond(carry):
            return (carry[3] < BISECT_MAX_ROUNDS) & (carry[4] == 0)

        def bis_round(carry):
            lo, hi, n_lo, rnd, _ = carry
            for _ in range(BISECT_STEPS_PER_ROUND):
                mid = mean(lo, hi)
                lo, hi, n_lo = halve(lo, hi, n_lo, mid, count(lambda blk, k0: blk >= mid), True)
            return lo, hi, n_lo, rnd + 1, settled(lo, hi, n_lo)

        lo_f, _, _, _, _ = lax.while_loop(bis_cond, bis_round, (lo, hi, n_lo, jnp.int32(0), settled(lo, hi, n_lo)))
        tau = jnp.where(few, INT_MIN + 1, jnp.maximum(lo_f, INT_MIN + 1))
        n_ge = jnp.where(few, 0, count(lambda blk, k0: blk >= tau))

        @pl.when(jnp.max(n_ge) > topk)
        def _():
            need = topk - count(lambda blk, k0: blk > tau)
            pos = lax.broadcasted_iota(jnp.int32, (cr, tq), 0)
            pos_bits = max(1, (seq - 1).bit_length())

            def pos_iter(it, last):
                cand = last + jnp.left_shift(jnp.int32(1), pos_bits - 1 - it)
                below = count(lambda blk, k0: jnp.where(blk == tau, pos + k0, cand) < cand)
                return jnp.where(below < need, cand, last)

            last = lax.fori_loop(0, pos_bits, pos_iter, jnp.zeros((1, tq), jnp.int32))

            def demote(j, carry):
                k0 = pl.multiple_of(j * cr, cr)
                blk = keys_ref[pl.ds(k0, cr), :]
                drop = jnp.where(blk == tau, pos + k0, last) > last
                keys_ref[pl.ds(k0, cr), :] = jnp.where(drop, tau - 1, blk)
                return carry

            lax.fori_loop(0, n_ct, demote, 0)

        def to_bias(j, carry):
            k0 = pl.multiple_of(j * cr, cr)
            bias_ref[pl.ds(k0, cr), :] = jnp.where(keys_ref[pl.ds(k0, cr), :] >= tau, 0.0, NEG_BIG).astype(BF16)
            return carry

        lax.fori_loop(0, n_ct, to_bias, 0)

        eye = (lax.broadcasted_iota(jnp.int32, (tq, tq), 0) == lax.broadcasted_iota(jnp.int32, (tq, tq), 1))
        eye = jnp.concatenate([jnp.where(eye, 1.0, 0.0).astype(BF16)] * grp, axis=1)
        for g in range(n_kv):
            qtg = jnp.concatenate([qt_ref[0, (g * grp + i) * HEAD_DIM:(g * grp + i + 1) * HEAD_DIM, :]
                                   for i in range(grp)], axis=1)
            qb_ref[g] = jnp.concatenate([qtg, eye], axis=0)


def _dsa(qt, qit, wt, k2, k, vt, topk):
    bsz, aw, seq = qt.shape
    tq = LANES
    n_idx = wt.shape[1]
    kw = k.shape[2]
    n_kv = kw // HEAD_DIM
    grp = aw // kw
    once = pl.Buffered(1)
    nq = seq // tq
    cur = lambda b, i: (b, 0, jnp.minimum(i, nq - 1))
    return pl.pallas_call(
        functools.partial(_dsa_body, topk=topk),
        grid=(bsz, nq + 1),
        in_specs=[
            pl.BlockSpec((1, aw, tq), cur),
            pl.BlockSpec((1, qit.shape[1], tq), cur),
            pl.BlockSpec((1, n_idx, tq), cur),
            pl.BlockSpec((1, seq, 2 * IDX_DIM), lambda b, i: (b, 0, 0), pipeline_mode=once),
            pl.BlockSpec((1, seq, kw), lambda b, i: (b, 0, 0), pipeline_mode=once),
            pl.BlockSpec((1, kw, seq), lambda b, i: (b, 0, 0), pipeline_mode=once),
        ],
        out_specs=pl.BlockSpec((1, tq, aw), lambda b, i: (b, jnp.maximum(i - 1, 0), 0)),
        out_shape=jax.ShapeDtypeStruct((bsz, seq, aw), BF16),
        scratch_shapes=[
            pltpu.VMEM((seq, tq), jnp.int32),
            pltpu.VMEM((-(-topk // DSA_IDX_ROWS) * DSA_IDX_ROWS, tq), jnp.int32),
            pltpu.VMEM((seq, tq), BF16),
            pltpu.VMEM((n_idx // 2, 2 * IDX_DIM, 2 * tq), BF16),
            pltpu.VMEM((n_kv, HEAD_DIM + tq, grp * tq), BF16),
            pltpu.VMEM((n_kv, 1, grp * tq), F32),
            pltpu.VMEM((n_kv, HEAD_DIM + ONES_ROWS, grp * tq), F32),
            pltpu.VMEM((2, n_kv, DSA_ATT_ROWS, grp * tq), F32),
            pltpu.VMEM((2, n_kv, 1, grp * tq), F32),
        ],
        compiler_params=_params(("parallel", "arbitrary")),
        name="dsa",
    )(qt, qit, wt, k2, k, vt)


def _conv_silu(blk_ref, halo_ref, buf_ref, w_ref, b_ref, ts):
    blk = blk_ref[0]
    buf_ref[0:8, :] = halo_ref[...]
    buf_ref[8:8 + ts, :] = blk
    halo_ref[...] = blk[ts - 8:ts, :]
    out = b_ref[...] + w_ref[CONV_WIDTH - 1:CONV_WIDTH, :] * blk
    for k in range(CONV_WIDTH - 1):
        out = out + w_ref[k:k + 1, :] * buf_ref[pl.ds(8 - (CONV_WIDTH - 1) + k, ts), :]
    return _silu(out)


def _ssd_body(z_ref, x_ref, b_ref, c_ref, dt_ref, dtt_ref, alog_row_ref, alog_col_ref, dskip_ref, gn_ref,
              wx_ref, wb_ref, wc_ref, bx_ref, bb_ref, bc_ref, o_ref,
              ht_ref, hx_ref, hb_ref, hc_ref, bufx_ref, bufb_ref, bufc_ref):
    ts = x_ref.shape[1]
    hg = dt_ref.shape[3]
    hd = SSM_HEAD_DIM
    ll = SSD_L

    @pl.when(pl.program_id(2) == 0)
    def _():
        ht_ref[...] = jnp.zeros(ht_ref.shape, F32)
        hx_ref[...] = jnp.zeros(hx_ref.shape, F32)
        hb_ref[...] = jnp.zeros(hb_ref.shape, F32)
        hc_ref[...] = jnp.zeros(hc_ref.shape, F32)

    xs = _conv_silu(x_ref, hx_ref, bufx_ref, wx_ref, bx_ref, ts)
    bm = _conv_silu(b_ref, hb_ref, bufb_ref, wb_ref, bb_ref, ts)
    cm = _conv_silu(c_ref, hc_ref, bufc_ref, wc_ref, bc_ref, ts)

    a_row = -jnp.exp(alog_row_ref[0])
    a_col = -jnp.exp(alog_col_ref[0])
    ti = lax.broadcasted_iota(jnp.int32, (ll, ll), 0)
    si = lax.broadcasted_iota(jnp.int32, (ll, ll), 1)
    causal = si <= ti
    tril = jnp.where(causal, 1.0, 0.0).astype(F32)
    triu = jnp.where(ti <= si, 1.0, 0.0).astype(F32)
    lane = lax.broadcasted_iota(jnp.int32, (1, 2 * hd), 1)
    first = lane < hd

    for ci in range(ts // ll):
        rows = slice(ci * ll, (ci + 1) * ll)
        dt = dt_ref[0, 0, rows, :]
        dtt = dtt_ref[0, 0, :, rows]
        acol = jnp.dot(tril, dt * a_row, preferred_element_type=F32, precision=lax.Precision.HIGHEST)
        arow = jnp.dot(dtt * a_col, triu, preferred_element_type=F32, precision=lax.Precision.HIGHEST)
        alast = arow[:, ll - 1:ll]
        wrow = dtt * jnp.exp(alast - arow)
        elast = jnp.exp(alast)
        xc = xs[rows]
        bc = bm[rows]
        cc = cm[rows]
        bcb = bc.astype(BF16)
        ccb = cc.astype(BF16)
        cb = lax.dot_general(ccb, bcb, (((1,), (1,)), ((), ())), preferred_element_type=F32)
        bt = bc.T
        ys = []
        for p in range(hg // 2):
            xp = xc[:, p * 2 * hd:(p + 1) * 2 * hd].astype(BF16)
            hp = ht_ref[p]
            rhs = jnp.concatenate([xp, hp.astype(BF16)], axis=0)
            y2, s2, e2 = [], [], []
            for e in range(2):
                j = 2 * p + e
                bcol = jnp.broadcast_to(acol[:, j:j + 1], (ll, ll))
                dec = jnp.where(causal, jnp.exp(bcol - arow[j:j + 1, :]), 0.0)
                mm = cb * dec * dtt[j:j + 1, :]
                ecol = jnp.exp(jnp.broadcast_to(acol[:, j:j + 1], (ll, SSM_STATE)))
                lhs = jnp.concatenate([mm, cc * ecol], axis=1).astype(BF16)
                y2.append(jnp.dot(lhs, rhs, preferred_element_type=F32))
                s2.append(jnp.dot((bt * wrow[j:j + 1, :]).astype(BF16), xp, preferred_element_type=F32))
                e2.append(jnp.broadcast_to(elast[j:j + 1, :], (1, 2 * hd)))
            ys.append(jnp.where(first, y2[0], y2[1]))
            ht_ref[p] = hp * jnp.where(first, e2[0], e2[1]) + jnp.where(first, s2[0], s2[1])
        y = jnp.concatenate(ys, axis=1)
        y = (y + dskip_ref[0] * xc) * _silu(z_ref[0, rows, :])
        y = y * lax.rsqrt(jnp.mean(y * y, axis=-1, keepdims=True) + NORM_EPS)
        o_ref[0, rows, :] = (y * gn_ref[0]).astype(o_ref.dtype)


def _ssd(zx, dt, conv_w, conv_b, a_log, d_skip, g_norm):
    bsz, seq, _ = zx.shape
    heads = dt.shape[2]
    g = SSM_GROUPS
    hg = heads // g
    wid = heads * SSM_HEAD_DIM
    gw = wid // g
    n = SSM_STATE
    ts = SSD_TS
    dtg = dt.reshape(bsz, seq, g, hg).transpose(0, 2, 1, 3)
    dtt = dtg.transpose(0, 1, 3, 2)
    zb, xb, bb, cb = 0, wid // gw, 2 * wid // n, (2 * wid + g * n) // n
    cw_x, cw_b, cw_c = conv_w[:, :wid], conv_w[:, wid:wid + g * n], conv_w[:, wid + g * n:]
    cb_x, cb_b, cb_c = (conv_b[None, :wid], conv_b[None, wid:wid + g * n], conv_b[None, wid + g * n:])
    col = lambda off: (lambda b, gi, c: (b, c, off + gi))
    par = lambda b, gi, c: (0, gi)
    return pl.pallas_call(
        _ssd_body,
        grid=(bsz, g, seq // ts),
        in_specs=[
            pl.BlockSpec((1, ts, gw), col(zb)),
            pl.BlockSpec((1, ts, gw), col(xb)),
            pl.BlockSpec((1, ts, n), col(bb)),
            pl.BlockSpec((1, ts, n), col(cb)),
            pl.BlockSpec((1, 1, ts, hg), lambda b, gi, c: (b, gi, c, 0)),
            pl.BlockSpec((1, 1, hg, ts), lambda b, gi, c: (b, gi, 0, c)),
            pl.BlockSpec((1, 1, hg), lambda b, gi, c: (gi, 0, 0)),
            pl.BlockSpec((1, hg, 1), lambda b, gi, c: (gi, 0, 0)),
            pl.BlockSpec((1, 1, gw), lambda b, gi, c: (gi, 0, 0)),
            pl.BlockSpec((1, 1, gw), lambda b, gi, c: (gi, 0, 0)),
            pl.BlockSpec((CONV_WIDTH, gw), par),
            pl.BlockSpec((CONV_WIDTH, n), par),
            pl.BlockSpec((CONV_WIDTH, n), par),
            pl.BlockSpec((1, gw), par),
            pl.BlockSpec((1, n), par),
            pl.BlockSpec((1, n), par),
        ],
        out_specs=pl.BlockSpec((1, ts, gw), lambda b, gi, c: (b, c, gi)),
        out_shape=jax.ShapeDtypeStruct((bsz, seq, wid), BF16),
        scratch_shapes=[
            pltpu.VMEM((hg // 2, n, 2 * SSM_HEAD_DIM), F32),
            pltpu.VMEM((8, gw), F32), pltpu.VMEM((8, n), F32), pltpu.VMEM((8, n), F32),
            pltpu.VMEM((ts + 8, gw), F32), pltpu.VMEM((ts + 8, n), F32), pltpu.VMEM((ts + 8, n), F32),
        ],
        compiler_params=_params(("parallel", "parallel", "arbitrary")),
        name="ssd",
    )(zx, zx, zx, zx, dtg, dtt, a_log.reshape(g, 1, hg), a_log.reshape(g, hg, 1),
      jnp.repeat(d_skip, SSM_HEAD_DIM).reshape(g, 1, gw), g_norm.reshape(g, 1, gw),
      cw_x, cw_b, cw_c, cb_x, cb_b, cb_c)


def _merge_body(u_ref, ya_ref, ys_ref, wg0_ref, wg1_ref, wa_ref, ws_ref, o_ref):
    u = u_ref[...]
    g0 = jax.nn.sigmoid(jnp.dot(u, wg0_ref[...], preferred_element_type=F32))
    g1 = jax.nn.sigmoid(jnp.dot(u, wg1_ref[...], preferred_element_type=F32))
    a = jnp.dot(ya_ref[...], wa_ref[...], preferred_element_type=F32)
    s = jnp.dot(ys_ref[...], ws_ref[...], preferred_element_type=F32)
    o_ref[...] = (g0 * a + g1 * s).astype(o_ref.dtype)


def _merge(u, ya, ys, w_gate, w_a, w_s):
    m, d = u.shape
    tm, tn = MERGE_TM, MERGE_TN
    nj = d // tn
    return pl.pallas_call(
        _merge_body,
        grid=(m // tm, nj),
        in_specs=[
            pl.BlockSpec((tm, d), lambda i, j: (i, 0)),
            pl.BlockSpec((tm, ya.shape[1]), lambda i, j: (i, 0)),
            pl.BlockSpec((tm, ys.shape[1]), lambda i, j: (i, 0)),
            pl.BlockSpec((d, tn), lambda i, j: (0, j)),
            pl.BlockSpec((d, tn), lambda i, j: (0, nj + j)),
            pl.BlockSpec((w_a.shape[0], tn), lambda i, j: (0, j)),
            pl.BlockSpec((w_s.shape[0], tn), lambda i, j: (0, j)),
        ],
        out_specs=pl.BlockSpec((tm, tn), lambda i, j: (i, j)),
        out_shape=jax.ShapeDtypeStruct((m, d), BF16),
        compiler_params=_params(("parallel", "arbitrary")),
        name="merge",
    )(u, ya, ys, w_gate, w_gate, w_a, w_s)


def _out_body(mg_ref, x_ref, w_ref, g_ref, h_ref, u_ref):
    h = x_ref[...] + jnp.dot(mg_ref[...], w_ref[...], preferred_element_type=F32)
    h_ref[...] = h
    y = h * lax.rsqrt(jnp.mean(h * h, axis=-1, keepdims=True) + NORM_EPS)
    u_ref[...] = (y * g_ref[...]).astype(u_ref.dtype)


def _out_proj(merged, x2, w_out, g_ffn):
    m, d = x2.shape
    tm = OUT_TM
    row = pl.BlockSpec((tm, d), lambda i: (i, 0))
    return pl.pallas_call(
        _out_body,
        grid=(m // tm,),
        in_specs=[row, row, pl.BlockSpec((d, d), lambda i: (0, 0)), pl.BlockSpec((1, d), lambda i: (0, 0))],
        out_specs=[row, row],
        out_shape=[jax.ShapeDtypeStruct((m, d), F32), jax.ShapeDtypeStruct((m, d), BF16)],
        compiler_params=_params(("parallel",)),
        name="out_proj",
    )(merged, x2, w_out, g_ffn.reshape(1, d))


def _ffn_up_body(u_ref, wg_ref, wu_ref, o_ref):
    u = u_ref[...]
    gate = jnp.dot(u, wg_ref[...], preferred_element_type=F32)
    up = jnp.dot(u, wu_ref[...], preferred_element_type=F32)
    o_ref[...] = (_silu(gate) * up).astype(o_ref.dtype)


def _ffn_up(u2, w_in):
    m, d = u2.shape
    dff = w_in.shape[1] // 2
    tm, tn = FFN_TM, FFN_TN
    nj = dff // tn
    return pl.pallas_call(
        _ffn_up_body,
        grid=(m // tm, nj),
        in_specs=[pl.BlockSpec((tm, d), lambda i, j: (i, 0)),
                  pl.BlockSpec((d, tn), lambda i, j: (0, j)),
                  pl.BlockSpec((d, tn), lambda i, j: (0, nj + j))],
        out_specs=pl.BlockSpec((tm, tn), lambda i, j: (i, j)),
        out_shape=jax.ShapeDtypeStruct((m, dff), BF16),
        compiler_params=_params(("parallel", "arbitrary")),
        name="ffn_up",
    )(u2, w_in, w_in)


def _ffn_down_body(a_ref, w_ref, h_ref, g_ref, o_ref):
    h = h_ref[...] + jnp.dot(a_ref[...], w_ref[...], preferred_element_type=F32)
    y = h * lax.rsqrt(jnp.mean(h * h, axis=-1, keepdims=True) + NORM_EPS)
    o_ref[...] = y * g_ref[...]


def _ffn_down(act, w_out, h1, g_final):
    m, dff = act.shape
    d = w_out.shape[1]
    tm = DOWN_TM
    return pl.pallas_call(
        _ffn_down_body,
        grid=(m // tm,),
        in_specs=[pl.BlockSpec((tm, dff), lambda i: (i, 0)),
                  pl.BlockSpec((dff, d), lambda i: (0, 0), pipeline_mode=pl.Buffered(1)),
                  pl.BlockSpec((tm, d), lambda i: (i, 0)),
                  pl.BlockSpec((1, d), lambda i: (0, 0))],
        out_specs=pl.BlockSpec((tm, d), lambda i: (i, 0)),
        out_shape=jax.ShapeDtypeStruct((m, d), F32),
        compiler_params=_params(("parallel",)),
        name="ffn_down",
    )(act, w_out, h1, g_final.reshape(1, d))


def _layer(h, w_in, w_gate, w_attn_branch, w_ssm_branch, w_out, conv_w, conv_b, dt_bias, a_log, d_skip,
           g_ssm_norm, g_mix, g_ffn, w_ffn_in, w_ffn_out, g_last):
    bsz, seq, d = h.shape
    m = bsz * seq
    aw = N_HEADS * HEAD_DIM
    kw = N_KV_HEADS * HEAD_DIM
    iw = N_IDX_HEADS * IDX_DIM
    sw = SSM_EXPAND * d
    heads = sw // SSM_HEAD_DIM
    bcw = SSM_GROUPS * SSM_STATE
    cuts = [0]
    for s in (aw, kw, kw, iw, IDX_DIM, N_IDX_HEADS, sw, sw + 2 * bcw, heads):
        cuts.append(cuts[-1] + s)
    wb = w_in.astype(BF16)
    seg = lambda i, j=None: wb[:, cuts[i]:cuts[i + 1 if j is None else j]]
    pad = lambda a, n: jnp.pad(a, ((0, 0), (0, n - a.shape[1])))
    w_small = jnp.concatenate([seg(4), seg(4), pad(seg(5), LANES), pad(seg(8), LANES)], axis=1)

    x2 = h.reshape(m, d)
    u = _rmsnorm(x2, g_mix)
    u3 = u.reshape(bsz, seq, d)

    tm = min(PROJ_TM, seq)
    cos_a, sin_a = _rope_tables(seq, HEAD_DIM)
    cos_i, sin_i = _rope_tables(seq, IDX_DIM)
    tab = _table_spec(tm)
    blk = lambda n: pl.BlockSpec((1, tm, n), lambda b, i, j: (b, i, j))
    blk_t = lambda n: pl.BlockSpec((1, n, tm), lambda b, i, j: (b, j, i))

    tn = min(PROJ_TN, aw)
    qt = _proj_call(u3, seg(0), functools.partial(_rope_t_epilogue, HEAD_DIM, HEAD_DIM ** -0.5 * math.log2(math.e)),
                    (cos_a, sin_a), (tab, tab), jax.ShapeDtypeStruct((bsz, aw, seq), BF16), blk_t(tn), tm, tn,
                    "proj_q")
    k, vt = _proj_call(u3, seg(1, 3), functools.partial(_kv_epilogue, kw), (cos_a, sin_a), (tab, tab),
                       [jax.ShapeDtypeStruct((bsz, seq, kw), BF16), jax.ShapeDtypeStruct((bsz, kw, seq), BF16)],
                       [pl.BlockSpec((1, tm, kw), lambda b, i, j: (b, i, 0)),
                        pl.BlockSpec((1, kw, tm), lambda b, i, j: (b, 0, i))], tm, 2 * kw, "proj_kv")
    tn = min(PROJ_TN, iw)
    qit = _proj_call(u3, seg(3), functools.partial(_rope_t_epilogue, IDX_DIM, 1.0),
                     (cos_i, sin_i), (tab, tab), jax.ShapeDtypeStruct((bsz, iw, seq), BF16), blk_t(tn), tm, tn,
                     "proj_qidx")
    k2, wt, dt = _proj_call(
        u3, w_small, functools.partial(_small_epilogue, N_IDX_HEADS, heads),
        (cos_i, sin_i, dt_bias.reshape(1, heads)),
        (tab, tab, pl.BlockSpec((1, heads), lambda b, i, j: (0, 0))),
        [jax.ShapeDtypeStruct((bsz, seq, 2 * IDX_DIM), BF16), jax.ShapeDtypeStruct((bsz, N_IDX_HEADS, seq), F32),
         jax.ShapeDtypeStruct((bsz, seq, heads), F32)],
        [pl.BlockSpec((1, tm, 2 * IDX_DIM), lambda b, i, j: (b, i, 0)),
         pl.BlockSpec((1, N_IDX_HEADS, tm), lambda b, i, j: (b, 0, i)),
         pl.BlockSpec((1, tm, heads), lambda b, i, j: (b, i, 0))], tm, 3 * LANES, "proj_small")
    nzx = 2 * sw + 2 * bcw
    tn = min(PROJ_TN, nzx)
    zx = _proj_call(u3, seg(6, 8), _plain_epilogue, (), (), jax.ShapeDtypeStruct((bsz, seq, nzx), F32),
                    blk(tn), tm, tn, "proj_zx")

    y_attn = _dsa(qt, qit, wt, k2, k, vt, min(TOPK_MAX, seq // 4))
    y_ssm = _ssd(zx, dt, conv_w, conv_b, a_log, d_skip, g_ssm_norm)

    merged = _merge(u, y_attn.reshape(m, aw), y_ssm.reshape(m, sw), w_gate.astype(BF16),
                    w_attn_branch.astype(BF16), w_ssm_branch.astype(BF16))
    h1, u2 = _out_proj(merged, x2, w_out.astype(BF16), g_ffn)
    act = _ffn_up(u2, w_ffn_in.astype(BF16))
    return _ffn_down(act, w_ffn_out.astype(BF16), h1, g_last).reshape(bsz, seq, d)


def kernel(x, w_in, w_gate, w_attn_branch, w_ssm_branch, w_out, conv_w, conv_b, dt_bias, a_log, d_skip, g_ssm_norm,
           g_mix, g_ffn, w_ffn_in, w_ffn_out, g_final):
    depth = w_in.shape[0]
    assert depth == 1, "the final norm is fused into the last layer's down projection"
    return _layer(x, w_in[0], w_gate[0], w_attn_branch[0], w_ssm_branch[0], w_out[0], conv_w[0], conv_b[0],
                  dt_bias[0], a_log[0], d_skip[0], g_ssm_norm[0], g_mix[0], g_ffn[0], w_ffn_in[0], w_ffn_out[0],
                  g_final)
```

```python
import functools
import math

import jax
import jax.numpy as jnp
from jax import lax
from jax.experimental import pallas as pl
from jax.experimental.pallas import tpu as pltpu

CHUNK = 64
ROPE_THETA = 10000.0
NORM_EPS = 1e-6
N_HEADS = 16
N_KV_HEADS = 4
HEAD_DIM = 128
TOPK_MAX = 256
N_IDX_HEADS = 16
IDX_DIM = 64
SSM_EXPAND = 2
SSM_HEAD_DIM = 64
SSM_GROUPS = 8
SSM_STATE = 128
CONV_WIDTH = 4

LANES = 128
VMEM_LIMIT = 60 * 1024 * 1024

PROJ_TM = 1024
PROJ_TN = 1024
DSA_IDX_ROWS = 128
DSA_STEP_ROWS = 1024
DSA_CNT_ROWS = 1024
DSA_ATT_ROWS = 256
ATT_TILES_PER_STEP = 4
CNT_ACC_ROWS = 32
ONES_ROWS = 16
BISECT_STEPS_PER_ROUND = 2
BISECT_MAX_ROUNDS = 17
SSD_TS = 256
SSD_L = 128
MERGE_TM = 512
MERGE_TN = 512
OUT_TM = 512
FFN_TM = 1024
FFN_TN = 512
DOWN_TM = 512

F32 = jnp.float32
BF16 = jnp.bfloat16
INT_MIN = -2147483648
NEG_BIG = -1e30


def _params(sem):
    return pltpu.CompilerParams(dimension_semantics=sem, vmem_limit_bytes=VMEM_LIMIT)


def _silu(x):
    h = 0.5 * x
    return h + h * jnp.tanh(h)


def _rms_body(x_ref, g_ref, o_ref):
    x = x_ref[...]
    y = x * lax.rsqrt(jnp.mean(x * x, axis=-1, keepdims=True) + NORM_EPS)
    o_ref[...] = (y * g_ref[...]).astype(o_ref.dtype)


def _rmsnorm(x2, g, tm=512):
    m, d = x2.shape
    return pl.pallas_call(
        _rms_body,
        grid=(m // tm,),
        in_specs=[pl.BlockSpec((tm, d), lambda i: (i, 0)), pl.BlockSpec((1, d), lambda i: (0, 0))],
        out_specs=pl.BlockSpec((tm, d), lambda i: (i, 0)),
        out_shape=jax.ShapeDtypeStruct((m, d), BF16),
        compiler_params=_params(("parallel",)),
        name="rmsnorm",
    )(x2, g.reshape(1, d))


def _rope_tables(seq, dim):
    pos = jnp.arange(seq, dtype=F32)
    inv = ROPE_THETA ** (-jnp.arange(0, dim, 2, dtype=F32) / dim)
    ang = pos[:, None] * inv[None, :]
    c, s = jnp.cos(ang), jnp.sin(ang)
    reps = LANES // dim
    return (jnp.tile(jnp.concatenate([c, c], axis=-1), (1, reps)),
            jnp.tile(jnp.concatenate([-s, s], axis=-1), (1, reps)))


def _swap_halves(x, dim):
    if dim == LANES:
        return pltpu.roll(x, LANES // 2, axis=1)
    lane = lax.broadcasted_iota(jnp.int32, x.shape, 1)
    first = (lane % dim) < (dim // 2)
    return jnp.where(first, pltpu.roll(x, LANES - dim // 2, axis=1), pltpu.roll(x, dim // 2, axis=1))


def _rope(x, cos_t, sin_t, dim):
    return x * cos_t + _swap_halves(x, dim) * sin_t


def _proj_call(u, w, epilogue, extras, extra_specs, out_shape, out_specs, tm, tn, name):
    bsz, seq, k = u.shape
    n = w.shape[1]

    def body(u_ref, w_ref, *refs):
        acc = jnp.dot(u_ref[0], w_ref[...], preferred_element_type=F32)
        epilogue(acc, *refs)

    return pl.pallas_call(
        body,
        grid=(bsz, seq // tm, n // tn),
        in_specs=[pl.BlockSpec((1, tm, k), lambda b, i, j: (b, i, 0)),
                  pl.BlockSpec((k, tn), lambda b, i, j: (0, j))] + list(extra_specs),
        out_specs=out_specs,
        out_shape=out_shape,
        compiler_params=_params(("parallel", "parallel", "arbitrary")),
        name=name,
    )(u, w, *extras)


def _table_spec(tm):
    return pl.BlockSpec((tm, LANES), lambda b, i, j: (i, 0))


def _rope_t_epilogue(dim, scale, acc, cos_ref, sin_ref, o_ref):
    c, s = cos_ref[...], sin_ref[...]
    for t in range(acc.shape[1] // LANES):
        sl = slice(t * LANES, (t + 1) * LANES)
        o_ref[0, sl, :] = (_rope(acc[:, sl], c, s, dim) * scale).T.astype(o_ref.dtype)


def _kv_epilogue(kw, acc, cos_ref, sin_ref, k_ref, vt_ref):
    c, s = cos_ref[...], sin_ref[...]
    for t in range(kw // LANES):
        sl = slice(t * LANES, (t + 1) * LANES)
        k_ref[0, :, sl] = _rope(acc[:, sl], c, s, HEAD_DIM).astype(k_ref.dtype)
        vt_ref[0, sl, :] = acc[:, kw + t * LANES:kw + (t + 1) * LANES].T.astype(vt_ref.dtype)


def _small_epilogue(n_idx, n_dt, acc, cos_ref, sin_ref, bias_ref, k2_ref, wt_ref, dt_ref):
    k2_ref[0] = _rope(acc[:, :LANES], cos_ref[...], sin_ref[...], IDX_DIM).astype(k2_ref.dtype)
    wt_ref[0] = acc[:, LANES:2 * LANES].T[:n_idx]
    x = acc[:, 2 * LANES:2 * LANES + n_dt] + bias_ref[...]
    dt_ref[0] = jnp.maximum(x, 0.0) + jnp.log1p(jnp.exp(-jnp.abs(x)))


def _plain_epilogue(acc, o_ref):
    o_ref[0] = acc.astype(o_ref.dtype)


def _to_key(x):
    bits = pltpu.bitcast(x, jnp.int32)
    return jnp.where(bits < 0, bits ^ jnp.int32(0x7FFFFFFF), bits)


def _dsa_body(qt_ref, qit_ref, wt_ref, k2_ref, k_ref, vt_ref, o_ref,
              keys_ref, slot_ref, rhs_ref, qb_ref, m_ref, acc_ref, s_ref, smax_ref, *, topk):
    tq = qt_ref.shape[2]
    n_idx = wt_ref.shape[1]
    seq = k_ref.shape[1]
    n_kv = k_ref.shape[2] // HEAD_DIM
    grp = qt_ref.shape[1] // HEAD_DIM // n_kv
    ir, ar = DSA_IDX_ROWS, DSA_ATT_ROWS
    cr = min(DSA_CNT_ROWS, seq)
    st = min(DSA_STEP_ROWS, cr)
    q0 = pl.program_id(1) * tq
    n_ct = (q0 + tq + cr - 1) // cr
    n_at = (q0 + tq + ar - 1) // ar

    zero = jnp.zeros((IDX_DIM, tq), rhs_ref.dtype)
    for p in range(n_idx // 2):
        top = qit_ref[0, 2 * p * IDX_DIM:(2 * p + 1) * IDX_DIM, :]
        bot = qit_ref[0, (2 * p + 1) * IDX_DIM:(2 * p + 2) * IDX_DIM, :]
        rhs_ref[p] = jnp.concatenate([jnp.concatenate([top, zero], axis=1),
                                      jnp.concatenate([zero, bot], axis=1)], axis=0)

    qpos = q0 + lax.broadcasted_iota(jnp.int32, (ir, tq), 1)
    limit = (qpos // CHUNK + 1) * CHUNK
    krow = lax.broadcasted_iota(jnp.int32, (ir, tq), 0)

    def idx_tile(j, carry):
        for sub in range(st // ir):
            k0 = pl.multiple_of(j * st + sub * ir, ir)
            kblk = k2_ref[0, pl.ds(k0, ir), :]
            acc = jnp.zeros((ir, tq), F32)
            for p in range(n_idx // 2):
                d = jnp.dot(kblk, rhs_ref[p], preferred_element_type=F32)
                acc = acc + jnp.maximum(d[:, :tq], 0.0) * wt_ref[0, 2 * p:2 * p + 1, :]
                acc = acc + jnp.maximum(d[:, tq:], 0.0) * wt_ref[0, 2 * p + 1:2 * p + 2, :]
            key = jnp.where(krow + k0 < limit, _to_key(acc), INT_MIN)
            keys_ref[pl.ds(k0, ir), :] = key
            s0 = pl.multiple_of(lax.rem(k0, slots), ir)
            slot_ref[pl.ds(s0, ir), :] = jnp.maximum(slot_ref[pl.ds(s0, ir), :], key)
        return carry

    slots = slot_ref.shape[0]
    slot_ref[...] = jnp.full(slot_ref.shape, INT_MIN, jnp.int32)
    lax.fori_loop(0, n_ct * (cr // st), idx_tile, 0)

    def count(pred):
        def cnt_tile(j, cnt):
            k0 = pl.multiple_of(j * cr, cr)
            hit = jnp.where(pred(keys_ref[pl.ds(k0, cr), :], k0), 1, 0)
            return cnt + jnp.sum(hit.reshape(cr // CNT_ACC_ROWS, CNT_ACC_ROWS, tq), axis=0)

        cnt = lax.fori_loop(0, n_ct, cnt_tile, jnp.zeros((CNT_ACC_ROWS, tq), jnp.int32))
        return jnp.sum(cnt, axis=0, keepdims=True)

    slot = slot_ref[...]
    lo = jnp.min(slot, axis=0, keepdims=True)
    hi = jnp.max(slot, axis=0, keepdims=True) + 1
    few = limit[:1, :] <= topk

    def settled(lo, hi, n_lo):
        ok = few | (n_lo == topk) | (hi - 1 <= lo)
        return jnp.min(jnp.where(ok, 1, 0))

    def bis_cond(carry):
        return (carry[3] < BISECT_MAX_ROUNDS) & (carry[4] == 0)

    def bis_round(carry):
        lo, hi, n_lo, rnd, _ = carry
        for _ in range(BISECT_STEPS_PER_ROUND):
            mid = (lo >> 1) + (hi >> 1) + (lo & hi & 1)
            total = count(lambda blk, k0: blk >= mid)
            ok = total >= topk
            lo, hi, n_lo = jnp.where(ok, mid, lo), jnp.where(ok, hi, mid), jnp.where(ok, total, n_lo)
        return lo, hi, n_lo, rnd + 1, settled(lo, hi, n_lo)

    n_lo = jnp.full((1, tq), topk + 1, jnp.int32)
    lo, hi, n_lo, _, _ = lax.while_loop(bis_cond, bis_round, (lo, hi, n_lo, jnp.int32(0), settled(lo, hi, n_lo)))
    tau = jnp.where(few, INT_MIN + 1, jnp.maximum(lo, INT_MIN + 1))
    n_ge = jnp.where(few, 0, count(lambda blk, k0: blk >= tau))

    @pl.when(jnp.max(n_ge) > topk)
    def _():
        need = topk - count(lambda blk, k0: blk > tau)
        pos = lax.broadcasted_iota(jnp.int32, (cr, tq), 0)
        pos_bits = max(1, (seq - 1).bit_length())

        def pos_iter(it, last):
            cand = last + jnp.left_shift(jnp.int32(1), pos_bits - 1 - it)
            below = count(lambda blk, k0: jnp.where(blk == tau, pos + k0, cand) < cand)
            return jnp.where(below < need, cand, last)

        last = lax.fori_loop(0, pos_bits, pos_iter, jnp.zeros((1, tq), jnp.int32))

        def demote(j, carry):
            k0 = pl.multiple_of(j * cr, cr)
            blk = keys_ref[pl.ds(k0, cr), :]
            drop = jnp.where(blk == tau, pos + k0, last) > last
            keys_ref[pl.ds(k0, cr), :] = jnp.where(drop, tau - 1, blk)
            return carry

        lax.fori_loop(0, n_ct, demote, 0)

    m_ref[...] = jnp.full(m_ref.shape, NEG_BIG, F32)
    acc_ref[...] = jnp.zeros(acc_ref.shape, F32)

    eye = (lax.broadcasted_iota(jnp.int32, (tq, tq), 0) == lax.broadcasted_iota(jnp.int32, (tq, tq), 1))
    eye = jnp.concatenate([jnp.where(eye, 1.0, 0.0).astype(BF16)] * grp, axis=1)
    for g in range(n_kv):
        qtg = jnp.concatenate([qt_ref[0, (g * grp + i) * HEAD_DIM:(g * grp + i + 1) * HEAD_DIM, :]
                               for i in range(grp)], axis=1)
        qb_ref[g] = jnp.concatenate([qtg, eye], axis=0)
    ones = jnp.ones((ONES_ROWS, ar), BF16)

    def logits_stage(j, slot):
        k0 = pl.multiple_of(jnp.minimum(j, n_at - 1) * ar, ar)
        tau_j = jnp.where(j < n_at, tau, jnp.iinfo(jnp.int32).max)
        bias = jnp.where(keys_ref[pl.ds(k0, ar), :] >= tau_j, 0.0, NEG_BIG).astype(BF16)
        for g in range(n_kv):
            kg = k_ref[0, pl.ds(k0, ar), g * HEAD_DIM:(g + 1) * HEAD_DIM]
            s = jnp.dot(jnp.concatenate([kg, bias], axis=1), qb_ref[g], preferred_element_type=F32)
            s_ref[slot, g] = s
            smax_ref[slot, g] = jnp.max(s, axis=0, keepdims=True)

    def softmax_stage(j, slot):
        k0 = pl.multiple_of(jnp.minimum(j, n_at - 1) * ar, ar)
        for g in range(n_kv):
            m_old = m_ref[g]
            m_new = jnp.maximum(m_old, smax_ref[slot, g])
            alpha = jnp.exp2(m_old - m_new)
            p = jnp.exp2(s_ref[slot, g] - m_new).astype(BF16)
            vtg = jnp.concatenate([vt_ref[0, g * HEAD_DIM:(g + 1) * HEAD_DIM, pl.ds(k0, ar)], ones], axis=0)
            acc_ref[g] = alpha * acc_ref[g] + jnp.dot(vtg, p, preferred_element_type=F32)
            m_ref[g] = m_new

    logits_stage(0, 0)

    def att_step(i, carry):
        j = ATT_TILES_PER_STEP * i
        for u in range(ATT_TILES_PER_STEP):
            logits_stage(j + u + 1, (u + 1) % 2)
            softmax_stage(j + u, u % 2)
        return carry

    lax.fori_loop(0, (n_at + ATT_TILES_PER_STEP - 1) // ATT_TILES_PER_STEP, att_step, 0)

    for g in range(n_kv):
        o = acc_ref[g, :HEAD_DIM, :] / acc_ref[g, HEAD_DIM:HEAD_DIM + 1, :]
        for i in range(grp):
            h = g * grp + i
            o_ref[0, :, h * HEAD_DIM:(h + 1) * HEAD_DIM] = o[:, i * tq:(i + 1) * tq].T.astype(o_ref.dtype)


def _dsa(qt, qit, wt, k2, k, vt, topk):
    bsz, aw, seq = qt.shape
    tq = LANES
    n_idx = wt.shape[1]
    kw = k.shape[2]
    n_kv = kw // HEAD_DIM
    grp = aw // kw
    once = pl.Buffered(1)
    return pl.pallas_call(
        functools.partial(_dsa_body, topk=topk),
        grid=(bsz, seq // tq),
        in_specs=[
            pl.BlockSpec((1, aw, tq), lambda b, i: (b, 0, i)),
            pl.BlockSpec((1, qit.shape[1], tq), lambda b, i: (b, 0, i)),
            pl.BlockSpec((1, n_idx, tq), lambda b, i: (b, 0, i)),
            pl.BlockSpec((1, seq, 2 * IDX_DIM), lambda b, i: (b, 0, 0), pipeline_mode=once),
            pl.BlockSpec((1, seq, kw), lambda b, i: (b, 0, 0), pipeline_mode=once),
            pl.BlockSpec((1, kw, seq), lambda b, i: (b, 0, 0), pipeline_mode=once),
        ],
        out_specs=pl.BlockSpec((1, tq, aw), lambda b, i: (b, i, 0)),
        out_shape=jax.ShapeDtypeStruct((bsz, seq, aw), BF16),
        scratch_shapes=[
            pltpu.VMEM((seq, tq), jnp.int32),
            pltpu.VMEM((-(-topk // DSA_IDX_ROWS) * DSA_IDX_ROWS, tq), jnp.int32),
            pltpu.VMEM((n_idx // 2, 2 * IDX_DIM, 2 * tq), BF16),
            pltpu.VMEM((n_kv, HEAD_DIM + tq, grp * tq), BF16),
            pltpu.VMEM((n_kv, 1, grp * tq), F32),
            pltpu.VMEM((n_kv, HEAD_DIM + ONES_ROWS, grp * tq), F32),
            pltpu.VMEM((2, n_kv, DSA_ATT_ROWS, grp * tq), F32),
            pltpu.VMEM((2, n_kv, 1, grp * tq), F32),
        ],
        compiler_params=_params(("parallel", "arbitrary")),
        name="dsa",
    )(qt, qit, wt, k2, k, vt)


def _conv_silu(blk_ref, halo_ref, buf_ref, w_ref, b_ref, ts):
    blk = blk_ref[0]
    buf_ref[0:8, :] = halo_ref[...]
    buf_ref[8:8 + ts, :] = blk
    halo_ref[...] = blk[ts - 8:ts, :]
    out = b_ref[...] + w_ref[CONV_WIDTH - 1:CONV_WIDTH, :] * blk
    for k in range(CONV_WIDTH - 1):
        out = out + w_ref[k:k + 1, :] * buf_ref[pl.ds(8 - (CONV_WIDTH - 1) + k, ts), :]
    return _silu(out)


def _ssd_body(z_ref, x_ref, b_ref, c_ref, dt_ref, dtt_ref, alog_row_ref, alog_col_ref, dskip_ref, gn_ref,
              wx_ref, wb_ref, wc_ref, bx_ref, bb_ref, bc_ref, o_ref,
              ht_ref, hx_ref, hb_ref, hc_ref, bufx_ref, bufb_ref, bufc_ref):
    ts = x_ref.shape[1]
    hg = dt_ref.shape[3]
    hd = SSM_HEAD_DIM
    ll = SSD_L

    @pl.when(pl.program_id(2) == 0)
    def _():
        ht_ref[...] = jnp.zeros(ht_ref.shape, F32)
        hx_ref[...] = jnp.zeros(hx_ref.shape, F32)
        hb_ref[...] = jnp.zeros(hb_ref.shape, F32)
        hc_ref[...] = jnp.zeros(hc_ref.shape, F32)

    xs = _conv_silu(x_ref, hx_ref, bufx_ref, wx_ref, bx_ref, ts)
    bm = _conv_silu(b_ref, hb_ref, bufb_ref, wb_ref, bb_ref, ts)
    cm = _conv_silu(c_ref, hc_ref, bufc_ref, wc_ref, bc_ref, ts)

    a_row = -jnp.exp(alog_row_ref[0])
    a_col = -jnp.exp(alog_col_ref[0])
    ti = lax.broadcasted_iota(jnp.int32, (ll, ll), 0)
    si = lax.broadcasted_iota(jnp.int32, (ll, ll), 1)
    causal = si <= ti
    tril = jnp.where(causal, 1.0, 0.0).astype(F32)
    triu = jnp.where(ti <= si, 1.0, 0.0).astype(F32)
    lane = lax.broadcasted_iota(jnp.int32, (1, 2 * hd), 1)
    first = lane < hd

    for ci in range(ts // ll):
        rows = slice(ci * ll, (ci + 1) * ll)
        dt = dt_ref[0, 0, rows, :]
        dtt = dtt_ref[0, 0, :, rows]
        acol = jnp.dot(tril, dt * a_row, preferred_element_type=F32, precision=lax.Precision.HIGHEST)
        arow = jnp.dot(dtt * a_col, triu, preferred_element_type=F32, precision=lax.Precision.HIGHEST)
        alast = arow[:, ll - 1:ll]
        wrow = dtt * jnp.exp(alast - arow)
        elast = jnp.exp(alast)
        xc = xs[rows]
        bc = bm[rows]
        cc = cm[rows]
        bcb = bc.astype(BF16)
        ccb = cc.astype(BF16)
        cb = lax.dot_general(ccb, bcb, (((1,), (1,)), ((), ())), preferred_element_type=F32)
        bt = bc.T
        ys = []
        for p in range(hg // 2):
            xp = xc[:, p * 2 * hd:(p + 1) * 2 * hd].astype(BF16)
            hp = ht_ref[p]
            rhs = jnp.concatenate([xp, hp.astype(BF16)], axis=0)
            y2, s2, e2 = [], [], []
            for e in range(2):
                j = 2 * p + e
                bcol = jnp.broadcast_to(acol[:, j:j + 1], (ll, ll))
                dec = jnp.where(causal, jnp.exp(bcol - arow[j:j + 1, :]), 0.0)
                mm = cb * dec * dtt[j:j + 1, :]
                ecol = jnp.exp(jnp.broadcast_to(acol[:, j:j + 1], (ll, SSM_STATE)))
                lhs = jnp.concatenate([mm, cc * ecol], axis=1).astype(BF16)
                y2.append(jnp.dot(lhs, rhs, preferred_element_type=F32))
                s2.append(jnp.dot((bt * wrow[j:j + 1, :]).astype(BF16), xp, preferred_element_type=F32))
                e2.append(jnp.broadcast_to(elast[j:j + 1, :], (1, 2 * hd)))
            ys.append(jnp.where(first, y2[0], y2[1]))
            ht_ref[p] = hp * jnp.where(first, e2[0], e2[1]) + jnp.where(first, s2[0], s2[1])
        y = jnp.concatenate(ys, axis=1)
        y = (y + dskip_ref[0] * xc) * _silu(z_ref[0, rows, :])
        y = y * lax.rsqrt(jnp.mean(y * y, axis=-1, keepdims=True) + NORM_EPS)
        o_ref[0, rows, :] = (y * gn_ref[0]).astype(o_ref.dtype)


def _ssd(zx, dt, conv_w, conv_b, a_log, d_skip, g_norm):
    bsz, seq, _ = zx.shape
    heads = dt.shape[2]
    g = SSM_GROUPS
    hg = heads // g
    wid = heads * SSM_HEAD_DIM
    gw = wid // g
    n = SSM_STATE
    ts = SSD_TS
    dtg = dt.reshape(bsz, seq, g, hg).transpose(0, 2, 1, 3)
    dtt = dtg.transpose(0, 1, 3, 2)
    zb, xb, bb, cb = 0, wid // gw, 2 * wid // n, (2 * wid + g * n) // n
    cw_x, cw_b, cw_c = conv_w[:, :wid], conv_w[:, wid:wid + g * n], conv_w[:, wid + g * n:]
    cb_x, cb_b, cb_c = (conv_b[None, :wid], conv_b[None, wid:wid + g * n], conv_b[None, wid + g * n:])
    col = lambda off: (lambda b, gi, c: (b, c, off + gi))
    par = lambda b, gi, c: (0, gi)
    return pl.pallas_call(
        _ssd_body,
        grid=(bsz, g, seq // ts),
        in_specs=[
            pl.BlockSpec((1, ts, gw), col(zb)),
            pl.BlockSpec((1, ts, gw), col(xb)),
            pl.BlockSpec((1, ts, n), col(bb)),
            pl.BlockSpec((1, ts, n), col(cb)),
            pl.BlockSpec((1, 1, ts, hg), lambda b, gi, c: (b, gi, c, 0)),
            pl.BlockSpec((1, 1, hg, ts), lambda b, gi, c: (b, gi, 0, c)),
            pl.BlockSpec((1, 1, hg), lambda b, gi, c: (gi, 0, 0)),
            pl.BlockSpec((1, hg, 1), lambda b, gi, c: (gi, 0, 0)),
            pl.BlockSpec((1, 1, gw), lambda b, gi, c: (gi, 0, 0)),
            pl.BlockSpec((1, 1, gw), lambda b, gi, c: (gi, 0, 0)),
            pl.BlockSpec((CONV_WIDTH, gw), par),
            pl.BlockSpec((CONV_WIDTH, n), par),
            pl.BlockSpec((CONV_WIDTH, n), par),
            pl.BlockSpec((1, gw), par),
            pl.BlockSpec((1, n), par),
            pl.BlockSpec((1, n), par),
        ],
        out_specs=pl.BlockSpec((1, ts, gw), lambda b, gi, c: (b, c, gi)),
        out_shape=jax.ShapeDtypeStruct((bsz, seq, wid), BF16),
        scratch_shapes=[
            pltpu.VMEM((hg // 2, n, 2 * SSM_HEAD_DIM), F32),
            pltpu.VMEM((8, gw), F32), pltpu.VMEM((8, n), F32), pltpu.VMEM((8, n), F32),
            pltpu.VMEM((ts + 8, gw), F32), pltpu.VMEM((ts + 8, n), F32), pltpu.VMEM((ts + 8, n), F32),
        ],
        compiler_params=_params(("parallel", "parallel", "arbitrary")),
        name="ssd",
    )(zx, zx, zx, zx, dtg, dtt, a_log.reshape(g, 1, hg), a_log.reshape(g, hg, 1),
      jnp.repeat(d_skip, SSM_HEAD_DIM).reshape(g, 1, gw), g_norm.reshape(g, 1, gw),
      cw_x, cw_b, cw_c, cb_x, cb_b, cb_c)


def _merge_body(u_ref, ya_ref, ys_ref, wg0_ref, wg1_ref, wa_ref, ws_ref, o_ref):
    u = u_ref[...]
    g0 = jax.nn.sigmoid(jnp.dot(u, wg0_ref[...], preferred_element_type=F32))
    g1 = jax.nn.sigmoid(jnp.dot(u, wg1_ref[...], preferred_element_type=F32))
    a = jnp.dot(ya_ref[...], wa_ref[...], preferred_element_type=F32)
    s = jnp.dot(ys_ref[...], ws_ref[...], preferred_element_type=F32)
    o_ref[...] = (g0 * a + g1 * s).astype(o_ref.dtype)


def _merge(u, ya, ys, w_gate, w_a, w_s):
    m, d = u.shape
    tm, tn = MERGE_TM, MERGE_TN
    nj = d // tn
    return pl.pallas_call(
        _merge_body,
        grid=(m // tm, nj),
        in_specs=[
            pl.BlockSpec((tm, d), lambda i, j: (i, 0)),
            pl.BlockSpec((tm, ya.shape[1]), lambda i, j: (i, 0)),
            pl.BlockSpec((tm, ys.shape[1]), lambda i, j: (i, 0)),
            pl.BlockSpec((d, tn), lambda i, j: (0, j)),
            pl.BlockSpec((d, tn), lambda i, j: (0, nj + j)),
            pl.BlockSpec((w_a.shape[0], tn), lambda i, j: (0, j)),
            pl.BlockSpec((w_s.shape[0], tn), lambda i, j: (0, j)),
        ],
        out_specs=pl.BlockSpec((tm, tn), lambda i, j: (i, j)),
        out_shape=jax.ShapeDtypeStruct((m, d), BF16),
        compiler_params=_params(("parallel", "arbitrary")),
        name="merge",
    )(u, ya, ys, w_gate, w_gate, w_a, w_s)


def _out_body(mg_ref, x_ref, w_ref, g_ref, h_ref, u_ref):
    h = x_ref[...] + jnp.dot(mg_ref[...], w_ref[...], preferred_element_type=F32)
    h_ref[...] = h
    y = h * lax.rsqrt(jnp.mean(h * h, axis=-1, keepdims=True) + NORM_EPS)
    u_ref[...] = (y * g_ref[...]).astype(u_ref.dtype)


def _out_proj(merged, x2, w_out, g_ffn):
    m, d = x2.shape
    tm = OUT_TM
    row = pl.BlockSpec((tm, d), lambda i: (i, 0))
    return pl.pallas_call(
        _out_body,
        grid=(m // tm,),
        in_specs=[row, row, pl.BlockSpec((d, d), lambda i: (0, 0)), pl.BlockSpec((1, d), lambda i: (0, 0))],
        out_specs=[row, row],
        out_shape=[jax.ShapeDtypeStruct((m, d), F32), jax.ShapeDtypeStruct((m, d), BF16)],
        compiler_params=_params(("parallel",)),
        name="out_proj",
    )(merged, x2, w_out, g_ffn.reshape(1, d))


def _ffn_up_body(u_ref, wg_ref, wu_ref, o_ref):
    u = u_ref[...]
    gate = jnp.dot(u, wg_ref[...], preferred_element_type=F32)
    up = jnp.dot(u, wu_ref[...], preferred_element_type=F32)
    o_ref[...] = (_silu(gate) * up).astype(o_ref.dtype)


def _ffn_up(u2, w_in):
    m, d = u2.shape
    dff = w_in.shape[1] // 2
    tm, tn = FFN_TM, FFN_TN
    nj = dff // tn
    return pl.pallas_call(
        _ffn_up_body,
        grid=(m // tm, nj),
        in_specs=[pl.BlockSpec((tm, d), lambda i, j: (i, 0)),
                  pl.BlockSpec((d, tn), lambda i, j: (0, j)),
                  pl.BlockSpec((d, tn), lambda i, j: (0, nj + j))],
        out_specs=pl.BlockSpec((tm, tn), lambda i, j: (i, j)),
        out_shape=jax.ShapeDtypeStruct((m, dff), BF16),
        compiler_params=_params(("parallel", "arbitrary")),
        name="ffn_up",
    )(u2, w_in, w_in)


def _ffn_down_body(a_ref, w_ref, h_ref, g_ref, o_ref):
    h = h_ref[...] + jnp.dot(a_ref[...], w_ref[...], preferred_element_type=F32)
    y = h * lax.rsqrt(jnp.mean(h * h, axis=-1, keepdims=True) + NORM_EPS)
    o_ref[...] = y * g_ref[...]


def _ffn_down(act, w_out, h1, g_final):
    m, dff = act.shape
    d = w_out.shape[1]
    tm = DOWN_TM
    return pl.pallas_call(
        _ffn_down_body,
        grid=(m // tm,),
        in_specs=[pl.BlockSpec((tm, dff), lambda i: (i, 0)),
                  pl.BlockSpec((dff, d), lambda i: (0, 0), pipeline_mode=pl.Buffered(1)),
                  pl.BlockSpec((tm, d), lambda i: (i, 0)),
                  pl.BlockSpec((1, d), lambda i: (0, 0))],
        out_specs=pl.BlockSpec((tm, d), lambda i: (i, 0)),
        out_shape=jax.ShapeDtypeStruct((m, d), F32),
        compiler_params=_params(("parallel",)),
        name="ffn_down",
    )(act, w_out, h1, g_final.reshape(1, d))


def _layer(h, w_in, w_gate, w_attn_branch, w_ssm_branch, w_out, conv_w, conv_b, dt_bias, a_log, d_skip,
           g_ssm_norm, g_mix, g_ffn, w_ffn_in, w_ffn_out, g_last):
    bsz, seq, d = h.shape
    m = bsz * seq
    aw = N_HEADS * HEAD_DIM
    kw = N_KV_HEADS * HEAD_DIM
    iw = N_IDX_HEADS * IDX_DIM
    sw = SSM_EXPAND * d
    heads = sw // SSM_HEAD_DIM
    bcw = SSM_GROUPS * SSM_STATE
    cuts = [0]
    for s in (aw, kw, kw, iw, IDX_DIM, N_IDX_HEADS, sw, sw + 2 * bcw, heads):
        cuts.append(cuts[-1] + s)
    wb = w_in.astype(BF16)
    seg = lambda i, j=None: wb[:, cuts[i]:cuts[i + 1 if j is None else j]]
    pad = lambda a, n: jnp.pad(a, ((0, 0), (0, n - a.shape[1])))
    w_small = jnp.concatenate([seg(4), seg(4), pad(seg(5), LANES), pad(seg(8), LANES)], axis=1)

    x2 = h.reshape(m, d)
    u = _rmsnorm(x2, g_mix)
    u3 = u.reshape(bsz, seq, d)

    tm = min(PROJ_TM, seq)
    cos_a, sin_a = _rope_tables(seq, HEAD_DIM)
    cos_i, sin_i = _rope_tables(seq, IDX_DIM)
    tab = _table_spec(tm)
    blk = lambda n: pl.BlockSpec((1, tm, n), lambda b, i, j: (b, i, j))
    blk_t = lambda n: pl.BlockSpec((1, n, tm), lambda b, i, j: (b, j, i))

    tn = min(PROJ_TN, aw)
    qt = _proj_call(u3, seg(0), functools.partial(_rope_t_epilogue, HEAD_DIM, HEAD_DIM ** -0.5 * math.log2(math.e)),
                    (cos_a, sin_a), (tab, tab), jax.ShapeDtypeStruct((bsz, aw, seq), BF16), blk_t(tn), tm, tn,
                    "proj_q")
    k, vt = _proj_call(u3, seg(1, 3), functools.partial(_kv_epilogue, kw), (cos_a, sin_a), (tab, tab),
                       [jax.ShapeDtypeStruct((bsz, seq, kw), BF16), jax.ShapeDtypeStruct((bsz, kw, seq), BF16)],
                       [pl.BlockSpec((1, tm, kw), lambda b, i, j: (b, i, 0)),
                        pl.BlockSpec((1, kw, tm), lambda b, i, j: (b, 0, i))], tm, 2 * kw, "proj_kv")
    tn = min(PROJ_TN, iw)
    qit = _proj_call(u3, seg(3), functools.partial(_rope_t_epilogue, IDX_DIM, 1.0),
                     (cos_i, sin_i), (tab, tab), jax.ShapeDtypeStruct((bsz, iw, seq), BF16), blk_t(tn), tm, tn,
                     "proj_qidx")
    k2, wt, dt = _proj_call(
        u3, w_small, functools.partial(_small_epilogue, N_IDX_HEADS, heads),
        (cos_i, sin_i, dt_bias.reshape(1, heads)),
        (tab, tab, pl.BlockSpec((1, heads), lambda b, i, j: (0, 0))),
        [jax.ShapeDtypeStruct((bsz, seq, 2 * IDX_DIM), BF16), jax.ShapeDtypeStruct((bsz, N_IDX_HEADS, seq), F32),
         jax.ShapeDtypeStruct((bsz, seq, heads), F32)],
        [pl.BlockSpec((1, tm, 2 * IDX_DIM), lambda b, i, j: (b, i, 0)),
         pl.BlockSpec((1, N_IDX_HEADS, tm), lambda b, i, j: (b, 0, i)),
         pl.BlockSpec((1, tm, heads), lambda b, i, j: (b, i, 0))], tm, 3 * LANES, "proj_small")
    nzx = 2 * sw + 2 * bcw
    tn = min(PROJ_TN, nzx)
    zx = _proj_call(u3, seg(6, 8), _plain_epilogue, (), (), jax.ShapeDtypeStruct((bsz, seq, nzx), F32),
                    blk(tn), tm, tn, "proj_zx")

    y_attn = _dsa(qt, qit, wt, k2, k, vt, min(TOPK_MAX, seq // 4))
    y_ssm = _ssd(zx, dt, conv_w, conv_b, a_log, d_skip, g_ssm_norm)

    merged = _merge(u, y_attn.reshape(m, aw), y_ssm.reshape(m, sw), w_gate.astype(BF16),
                    w_attn_branch.astype(BF16), w_ssm_branch.astype(BF16))
    h1, u2 = _out_proj(merged, x2, w_out.astype(BF16), g_ffn)
    act = _ffn_up(u2, w_ffn_in.astype(BF16))
    return _ffn_down(act, w_ffn_out.astype(BF16), h1, g_last).reshape(bsz, seq, d)


def kernel(x, w_in, w_gate, w_attn_branch, w_ssm_branch, w_out, conv_w, conv_b, dt_bias, a_log, d_skip, g_ssm_norm,
           g_mix, g_ffn, w_ffn_in, w_ffn_out, g_final):
    depth = w_in.shape[0]
    assert depth == 1, "the final norm is fused into the last layer's down projection"
    return _layer(x, w_in[0], w_gate[0], w_attn_branch[0], w_ssm_branch[0], w_out[0], conv_w[0], conv_b[0],
                  dt_bias[0], a_log[0], d_skip[0], g_ssm_norm[0], g_mix[0], g_ffn[0], w_ffn_in[0], w_ffn_out[0],
                  g_final)
```

```python
import functools
import math

import jax
import jax.numpy as jnp
from jax import lax
from jax.experimental import pallas as pl
from jax.experimental.pallas import tpu as pltpu

CHUNK = 64
ROPE_THETA = 10000.0
NORM_EPS = 1e-6
N_HEADS = 16
N_KV_HEADS = 4
HEAD_DIM = 128
TOPK_MAX = 256
N_IDX_HEADS = 16
IDX_DIM = 64
SSM_EXPAND = 2
SSM_HEAD_DIM = 64
SSM_GROUPS = 8
SSM_STATE = 128
CONV_WIDTH = 4

LANES = 128
VMEM_LIMIT = 60 * 1024 * 1024

PROJ_TM = 1024
PROJ_TN = 1024
DSA_IDX_ROWS = 128
DSA_STEP_ROWS = 1024
DSA_CNT_ROWS = 1024
DSA_ATT_ROWS = 256
ATT_TILES_PER_STEP = 4
CNT_ACC_ROWS = 32
ONES_ROWS = 16
BISECT_STEPS_PER_ROUND = 2
BISECT_MAX_ROUNDS = 17
SSD_TS = 1024
SSD_L = 128
MERGE_TM = 512
MERGE_TN = 512
OUT_TM = 512
FFN_TM = 1024
FFN_TN = 512
DOWN_TM = 512

F32 = jnp.float32
BF16 = jnp.bfloat16
INT_MIN = -2147483648
NEG_BIG = -1e30


def _params(sem):
    return pltpu.CompilerParams(dimension_semantics=sem, vmem_limit_bytes=VMEM_LIMIT)


def _silu(x):
    h = 0.5 * x
    return h + h * jnp.tanh(h)


def _rms_body(x_ref, g_ref, o_ref):
    x = x_ref[...]
    y = x * lax.rsqrt(jnp.mean(x * x, axis=-1, keepdims=True) + NORM_EPS)
    o_ref[...] = (y * g_ref[...]).astype(o_ref.dtype)


def _rmsnorm(x2, g, tm=512):
    m, d = x2.shape
    return pl.pallas_call(
        _rms_body,
        grid=(m // tm,),
        in_specs=[pl.BlockSpec((tm, d), lambda i: (i, 0)), pl.BlockSpec((1, d), lambda i: (0, 0))],
        out_specs=pl.BlockSpec((tm, d), lambda i: (i, 0)),
        out_shape=jax.ShapeDtypeStruct((m, d), BF16),
        compiler_params=_params(("parallel",)),
        name="rmsnorm",
    )(x2, g.reshape(1, d))


def _rope_tables(seq, dim):
    pos = jnp.arange(seq, dtype=F32)
    inv = ROPE_THETA ** (-jnp.arange(0, dim, 2, dtype=F32) / dim)
    ang = pos[:, None] * inv[None, :]
    c, s = jnp.cos(ang), jnp.sin(ang)
    reps = LANES // dim
    return (jnp.tile(jnp.concatenate([c, c], axis=-1), (1, reps)),
            jnp.tile(jnp.concatenate([-s, s], axis=-1), (1, reps)))


def _swap_halves(x, dim):
    if dim == LANES:
        return pltpu.roll(x, LANES // 2, axis=1)
    lane = lax.broadcasted_iota(jnp.int32, x.shape, 1)
    first = (lane % dim) < (dim // 2)
    return jnp.where(first, pltpu.roll(x, LANES - dim // 2, axis=1), pltpu.roll(x, dim // 2, axis=1))


def _rope(x, cos_t, sin_t, dim):
    return x * cos_t + _swap_halves(x, dim) * sin_t


def _proj_call(u, w, epilogue, extras, extra_specs, out_shape, out_specs, tm, tn, name):
    bsz, seq, k = u.shape
    n = w.shape[1]

    def body(u_ref, w_ref, *refs):
        acc = jnp.dot(u_ref[0], w_ref[...], preferred_element_type=F32)
        epilogue(acc, *refs)

    return pl.pallas_call(
        body,
        grid=(bsz, seq // tm, n // tn),
        in_specs=[pl.BlockSpec((1, tm, k), lambda b, i, j: (b, i, 0)),
                  pl.BlockSpec((k, tn), lambda b, i, j: (0, j))] + list(extra_specs),
        out_specs=out_specs,
        out_shape=out_shape,
        compiler_params=_params(("parallel", "parallel", "arbitrary")),
        name=name,
    )(u, w, *extras)


def _table_spec(tm):
    return pl.BlockSpec((tm, LANES), lambda b, i, j: (i, 0))


def _rope_t_epilogue(dim, scale, acc, cos_ref, sin_ref, o_ref):
    c, s = cos_ref[...], sin_ref[...]
    for t in range(acc.shape[1] // LANES):
        sl = slice(t * LANES, (t + 1) * LANES)
        o_ref[0, sl, :] = (_rope(acc[:, sl], c, s, dim) * scale).T.astype(o_ref.dtype)


def _kv_epilogue(kw, acc, cos_ref, sin_ref, k_ref, vt_ref):
    c, s = cos_ref[...], sin_ref[...]
    for t in range(kw // LANES):
        sl = slice(t * LANES, (t + 1) * LANES)
        k_ref[0, :, sl] = _rope(acc[:, sl], c, s, HEAD_DIM).astype(k_ref.dtype)
        vt_ref[0, sl, :] = acc[:, kw + t * LANES:kw + (t + 1) * LANES].T.astype(vt_ref.dtype)


def _small_epilogue(n_idx, n_dt, acc, cos_ref, sin_ref, bias_ref, k2_ref, wt_ref, dt_ref):
    k2_ref[0] = _rope(acc[:, :LANES], cos_ref[...], sin_ref[...], IDX_DIM).astype(k2_ref.dtype)
    wt_ref[0] = acc[:, LANES:2 * LANES].T[:n_idx]
    x = acc[:, 2 * LANES:2 * LANES + n_dt] + bias_ref[...]
    dt_ref[0] = jnp.maximum(x, 0.0) + jnp.log1p(jnp.exp(-jnp.abs(x)))


def _plain_epilogue(acc, o_ref):
    o_ref[0] = acc.astype(o_ref.dtype)


def _to_key(x):
    bits = pltpu.bitcast(x, jnp.int32)
    return jnp.where(bits < 0, bits ^ jnp.int32(0x7FFFFFFF), bits)


def _dsa_body(qt_ref, qit_ref, wt_ref, k2_ref, k_ref, vt_ref, o_ref,
              keys_ref, slot_ref, rhs_ref, qb_ref, m_ref, acc_ref, s_ref, smax_ref, *, topk):
    tq = qt_ref.shape[2]
    n_idx = wt_ref.shape[1]
    seq = k_ref.shape[1]
    n_kv = k_ref.shape[2] // HEAD_DIM
    grp = qt_ref.shape[1] // HEAD_DIM // n_kv
    ir, ar = DSA_IDX_ROWS, DSA_ATT_ROWS
    cr = min(DSA_CNT_ROWS, seq)
    st = min(DSA_STEP_ROWS, cr)
    q0 = pl.program_id(1) * tq
    n_ct = (q0 + tq + cr - 1) // cr
    n_at = (q0 + tq + ar - 1) // ar

    zero = jnp.zeros((IDX_DIM, tq), rhs_ref.dtype)
    for p in range(n_idx // 2):
        top = qit_ref[0, 2 * p * IDX_DIM:(2 * p + 1) * IDX_DIM, :]
        bot = qit_ref[0, (2 * p + 1) * IDX_DIM:(2 * p + 2) * IDX_DIM, :]
        rhs_ref[p] = jnp.concatenate([jnp.concatenate([top, zero], axis=1),
                                      jnp.concatenate([zero, bot], axis=1)], axis=0)

    qpos = q0 + lax.broadcasted_iota(jnp.int32, (ir, tq), 1)
    limit = (qpos // CHUNK + 1) * CHUNK
    krow = lax.broadcasted_iota(jnp.int32, (ir, tq), 0)

    def idx_tile(j, carry):
        for sub in range(st // ir):
            k0 = pl.multiple_of(j * st + sub * ir, ir)
            kblk = k2_ref[0, pl.ds(k0, ir), :]
            acc = jnp.zeros((ir, tq), F32)
            for p in range(n_idx // 2):
                d = jnp.dot(kblk, rhs_ref[p], preferred_element_type=F32)
                acc = acc + jnp.maximum(d[:, :tq], 0.0) * wt_ref[0, 2 * p:2 * p + 1, :]
                acc = acc + jnp.maximum(d[:, tq:], 0.0) * wt_ref[0, 2 * p + 1:2 * p + 2, :]
            key = jnp.where(krow + k0 < limit, _to_key(acc), INT_MIN)
            keys_ref[pl.ds(k0, ir), :] = key
            s0 = pl.multiple_of(lax.rem(k0, slots), ir)
            slot_ref[pl.ds(s0, ir), :] = jnp.maximum(slot_ref[pl.ds(s0, ir), :], key)
        return carry

    slots = slot_ref.shape[0]
    slot_ref[...] = jnp.full(slot_ref.shape, INT_MIN, jnp.int32)
    lax.fori_loop(0, n_ct * (cr // st), idx_tile, 0)

    def count(pred):
        def cnt_tile(j, cnt):
            k0 = pl.multiple_of(j * cr, cr)
            hit = jnp.where(pred(keys_ref[pl.ds(k0, cr), :], k0), 1, 0)
            return cnt + jnp.sum(hit.reshape(cr // CNT_ACC_ROWS, CNT_ACC_ROWS, tq), axis=0)

        cnt = lax.fori_loop(0, n_ct, cnt_tile, jnp.zeros((CNT_ACC_ROWS, tq), jnp.int32))
        return jnp.sum(cnt, axis=0, keepdims=True)

    slot = slot_ref[...]
    lo = jnp.min(slot, axis=0, keepdims=True)
    hi = jnp.max(slot, axis=0, keepdims=True) + 1
    few = limit[:1, :] <= topk

    def settled(lo, hi, n_lo):
        ok = few | (n_lo == topk) | (hi - 1 <= lo)
        return jnp.min(jnp.where(ok, 1, 0))

    def bis_cond(carry):
        return (carry[3] < BISECT_MAX_ROUNDS) & (carry[4] == 0)

    def bis_round(carry):
        lo, hi, n_lo, rnd, _ = carry
        for _ in range(BISECT_STEPS_PER_ROUND):
            mid = (lo >> 1) + (hi >> 1) + (lo & hi & 1)
            total = count(lambda blk, k0: blk >= mid)
            ok = total >= topk
            lo, hi, n_lo = jnp.where(ok, mid, lo), jnp.where(ok, hi, mid), jnp.where(ok, total, n_lo)
        return lo, hi, n_lo, rnd + 1, settled(lo, hi, n_lo)

    n_lo = jnp.full((1, tq), topk + 1, jnp.int32)
    lo, hi, n_lo, _, _ = lax.while_loop(bis_cond, bis_round, (lo, hi, n_lo, jnp.int32(0), settled(lo, hi, n_lo)))
    tau = jnp.where(few, INT_MIN + 1, jnp.maximum(lo, INT_MIN + 1))
    n_ge = jnp.where(few, 0, count(lambda blk, k0: blk >= tau))

    @pl.when(jnp.max(n_ge) > topk)
    def _():
        need = topk - count(lambda blk, k0: blk > tau)
        pos = lax.broadcasted_iota(jnp.int32, (cr, tq), 0)
        pos_bits = max(1, (seq - 1).bit_length())

        def pos_iter(it, last):
            cand = last + jnp.left_shift(jnp.int32(1), pos_bits - 1 - it)
            below = count(lambda blk, k0: jnp.where(blk == tau, pos + k0, cand) < cand)
            return jnp.where(below < need, cand, last)

        last = lax.fori_loop(0, pos_bits, pos_iter, jnp.zeros((1, tq), jnp.int32))

        def demote(j, carry):
            k0 = pl.multiple_of(j * cr, cr)
            blk = keys_ref[pl.ds(k0, cr), :]
            drop = jnp.where(blk == tau, pos + k0, last) > last
            keys_ref[pl.ds(k0, cr), :] = jnp.where(drop, tau - 1, blk)
            return carry

        lax.fori_loop(0, n_ct, demote, 0)

    m_ref[...] = jnp.full(m_ref.shape, NEG_BIG, F32)
    acc_ref[...] = jnp.zeros(acc_ref.shape, F32)

    eye = (lax.broadcasted_iota(jnp.int32, (tq, tq), 0) == lax.broadcasted_iota(jnp.int32, (tq, tq), 1))
    eye = jnp.concatenate([jnp.where(eye, 1.0, 0.0).astype(BF16)] * grp, axis=1)
    for g in range(n_kv):
        qtg = jnp.concatenate([qt_ref[0, (g * grp + i) * HEAD_DIM:(g * grp + i + 1) * HEAD_DIM, :]
                               for i in range(grp)], axis=1)
        qb_ref[g] = jnp.concatenate([qtg, eye], axis=0)
    ones = jnp.ones((ONES_ROWS, ar), BF16)

    def logits_stage(j, slot):
        k0 = pl.multiple_of(jnp.minimum(j, n_at - 1) * ar, ar)
        tau_j = jnp.where(j < n_at, tau, jnp.iinfo(jnp.int32).max)
        bias = jnp.where(keys_ref[pl.ds(k0, ar), :] >= tau_j, 0.0, NEG_BIG).astype(BF16)
        for g in range(n_kv):
            kg = k_ref[0, pl.ds(k0, ar), g * HEAD_DIM:(g + 1) * HEAD_DIM]
            s = jnp.dot(jnp.concatenate([kg, bias], axis=1), qb_ref[g], preferred_element_type=F32)
            s_ref[slot, g] = s
            smax_ref[slot, g] = jnp.max(s, axis=0, keepdims=True)

    def softmax_stage(j, slot):
        k0 = pl.multiple_of(jnp.minimum(j, n_at - 1) * ar, ar)
        for g in range(n_kv):
            m_old = m_ref[g]
            m_new = jnp.maximum(m_old, smax_ref[slot, g])
            alpha = jnp.exp2(m_old - m_new)
            p = jnp.exp2(s_ref[slot, g] - m_new).astype(BF16)
            vtg = jnp.concatenate([vt_ref[0, g * HEAD_DIM:(g + 1) * HEAD_DIM, pl.ds(k0, ar)], ones], axis=0)
            acc_ref[g] = alpha * acc_ref[g] + jnp.dot(vtg, p, preferred_element_type=F32)
            m_ref[g] = m_new

    logits_stage(0, 0)

    def att_step(i, carry):
        j = ATT_TILES_PER_STEP * i
        for u in range(ATT_TILES_PER_STEP):
            logits_stage(j + u + 1, (u + 1) % 2)
            softmax_stage(j + u, u % 2)
        return carry

    lax.fori_loop(0, (n_at + ATT_TILES_PER_STEP - 1) // ATT_TILES_PER_STEP, att_step, 0)

    for g in range(n_kv):
        o = acc_ref[g, :HEAD_DIM, :] / acc_ref[g, HEAD_DIM:HEAD_DIM + 1, :]
        for i in range(grp):
            h = g * grp + i
            o_ref[0, :, h * HEAD_DIM:(h + 1) * HEAD_DIM] = o[:, i * tq:(i + 1) * tq].T.astype(o_ref.dtype)


def _dsa(qt, qit, wt, k2, k, vt, topk):
    bsz, aw, seq = qt.shape
    tq = LANES
    n_idx = wt.shape[1]
    kw = k.shape[2]
    n_kv = kw // HEAD_DIM
    grp = aw // kw
    once = pl.Buffered(1)
    return pl.pallas_call(
        functools.partial(_dsa_body, topk=topk),
        grid=(bsz, seq // tq),
        in_specs=[
            pl.BlockSpec((1, aw, tq), lambda b, i: (b, 0, i)),
            pl.BlockSpec((1, qit.shape[1], tq), lambda b, i: (b, 0, i)),
            pl.BlockSpec((1, n_idx, tq), lambda b, i: (b, 0, i)),
            pl.BlockSpec((1, seq, 2 * IDX_DIM), lambda b, i: (b, 0, 0), pipeline_mode=once),
            pl.BlockSpec((1, seq, kw), lambda b, i: (b, 0, 0), pipeline_mode=once),
            pl.BlockSpec((1, kw, seq), lambda b, i: (b, 0, 0), pipeline_mode=once),
        ],
        out_specs=pl.BlockSpec((1, tq, aw), lambda b, i: (b, i, 0)),
        out_shape=jax.ShapeDtypeStruct((bsz, seq, aw), BF16),
        scratch_shapes=[
            pltpu.VMEM((seq, tq), jnp.int32),
            pltpu.VMEM((-(-topk // DSA_IDX_ROWS) * DSA_IDX_ROWS, tq), jnp.int32),
            pltpu.VMEM((n_idx // 2, 2 * IDX_DIM, 2 * tq), BF16),
            pltpu.VMEM((n_kv, HEAD_DIM + tq, grp * tq), BF16),
            pltpu.VMEM((n_kv, 1, grp * tq), F32),
            pltpu.VMEM((n_kv, HEAD_DIM + ONES_ROWS, grp * tq), F32),
            pltpu.VMEM((2, n_kv, DSA_ATT_ROWS, grp * tq), F32),
            pltpu.VMEM((2, n_kv, 1, grp * tq), F32),
        ],
        compiler_params=_params(("parallel", "arbitrary")),
        name="dsa",
    )(qt, qit, wt, k2, k, vt)


def _conv_silu(blk_ref, halo_ref, buf_ref, w_ref, b_ref, ts):
    blk = blk_ref[0]
    buf_ref[0:8, :] = halo_ref[...]
    buf_ref[8:8 + ts, :] = blk
    halo_ref[...] = blk[ts - 8:ts, :]
    out = b_ref[...] + w_ref[CONV_WIDTH - 1:CONV_WIDTH, :] * blk
    for k in range(CONV_WIDTH - 1):
        out = out + w_ref[k:k + 1, :] * buf_ref[pl.ds(8 - (CONV_WIDTH - 1) + k, ts), :]
    return _silu(out)


def _ssd_body(z_ref, x_ref, b_ref, c_ref, dt_ref, dtt_ref, alog_row_ref, alog_col_ref, dskip_ref, gn_ref,
              wx_ref, wb_ref, wc_ref, bx_ref, bb_ref, bc_ref, o_ref,
              ht_ref, hx_ref, hb_ref, hc_ref, bufx_ref, bufb_ref, bufc_ref):
    ts = x_ref.shape[1]
    hg = dt_ref.shape[3]
    hd = SSM_HEAD_DIM
    ll = SSD_L

    @pl.when(pl.program_id(2) == 0)
    def _():
        ht_ref[...] = jnp.zeros(ht_ref.shape, F32)
        hx_ref[...] = jnp.zeros(hx_ref.shape, F32)
        hb_ref[...] = jnp.zeros(hb_ref.shape, F32)
        hc_ref[...] = jnp.zeros(hc_ref.shape, F32)

    xs = _conv_silu(x_ref, hx_ref, bufx_ref, wx_ref, bx_ref, ts)
    bm = _conv_silu(b_ref, hb_ref, bufb_ref, wb_ref, bb_ref, ts)
    cm = _conv_silu(c_ref, hc_ref, bufc_ref, wc_ref, bc_ref, ts)

    a_row = -jnp.exp(alog_row_ref[0])
    a_col = -jnp.exp(alog_col_ref[0])
    ti = lax.broadcasted_iota(jnp.int32, (ll, ll), 0)
    si = lax.broadcasted_iota(jnp.int32, (ll, ll), 1)
    causal = si <= ti
    tril = jnp.where(causal, 1.0, 0.0).astype(F32)
    triu = jnp.where(ti <= si, 1.0, 0.0).astype(F32)
    lane = lax.broadcasted_iota(jnp.int32, (1, 2 * hd), 1)
    first = lane < hd

    for ci in range(ts // ll):
        rows = slice(ci * ll, (ci + 1) * ll)
        dt = dt_ref[0, 0, rows, :]
        dtt = dtt_ref[0, 0, :, rows]
        acol = jnp.dot(tril, dt * a_row, preferred_element_type=F32, precision=lax.Precision.HIGHEST)
        arow = jnp.dot(dtt * a_col, triu, preferred_element_type=F32, precision=lax.Precision.HIGHEST)
        alast = arow[:, ll - 1:ll]
        wrow = dtt * jnp.exp(alast - arow)
        elast = jnp.exp(alast)
        xc = xs[rows]
        bc = bm[rows]
        cc = cm[rows]
        bcb = bc.astype(BF16)
        ccb = cc.astype(BF16)
        cb = lax.dot_general(ccb, bcb, (((1,), (1,)), ((), ())), preferred_element_type=F32)
        bt = bc.T
        ys = []
        for p in range(hg // 2):
            xp = xc[:, p * 2 * hd:(p + 1) * 2 * hd].astype(BF16)
            hp = ht_ref[p]
            rhs = jnp.concatenate([xp, hp.astype(BF16)], axis=0)
            y2, s2, e2 = [], [], []
            for e in range(2):
                j = 2 * p + e
                bcol = jnp.broadcast_to(acol[:, j:j + 1], (ll, ll))
                dec = jnp.where(causal, jnp.exp(bcol - arow[j:j + 1, :]), 0.0)
                mm = cb * dec * dtt[j:j + 1, :]
                ecol = jnp.exp(jnp.broadcast_to(acol[:, j:j + 1], (ll, SSM_STATE)))
                lhs = jnp.concatenate([mm, cc * ecol], axis=1).astype(BF16)
                y2.append(jnp.dot(lhs, rhs, preferred_element_type=F32))
                s2.append(jnp.dot((bt * wrow[j:j + 1, :]).astype(BF16), xp, preferred_element_type=F32))
                e2.append(jnp.broadcast_to(elast[j:j + 1, :], (1, 2 * hd)))
            ys.append(jnp.where(first, y2[0], y2[1]))
            ht_ref[p] = hp * jnp.where(first, e2[0], e2[1]) + jnp.where(first, s2[0], s2[1])
        y = jnp.concatenate(ys, axis=1)
        y = (y + dskip_ref[0] * xc) * _silu(z_ref[0, rows, :])
        y = y * lax.rsqrt(jnp.mean(y * y, axis=-1, keepdims=True) + NORM_EPS)
        o_ref[0, rows, :] = (y * gn_ref[0]).astype(o_ref.dtype)


def _ssd(zx, dt, conv_w, conv_b, a_log, d_skip, g_norm):
    bsz, seq, _ = zx.shape
    heads = dt.shape[2]
    g = SSM_GROUPS
    hg = heads // g
    wid = heads * SSM_HEAD_DIM
    gw = wid // g
    n = SSM_STATE
    ts = SSD_TS
    dtg = dt.reshape(bsz, seq, g, hg).transpose(0, 2, 1, 3)
    dtt = dtg.transpose(0, 1, 3, 2)
    zb, xb, bb, cb = 0, wid // gw, 2 * wid // n, (2 * wid + g * n) // n
    cw_x, cw_b, cw_c = conv_w[:, :wid], conv_w[:, wid:wid + g * n], conv_w[:, wid + g * n:]
    cb_x, cb_b, cb_c = (conv_b[None, :wid], conv_b[None, wid:wid + g * n], conv_b[None, wid + g * n:])
    col = lambda off: (lambda b, gi, c: (b, c, off + gi))
    par = lambda b, gi, c: (0, gi)
    return pl.pallas_call(
        _ssd_body,
        grid=(bsz, g, seq // ts),
        in_specs=[
            pl.BlockSpec((1, ts, gw), col(zb)),
            pl.BlockSpec((1, ts, gw), col(xb)),
            pl.BlockSpec((1, ts, n), col(bb)),
            pl.BlockSpec((1, ts, n), col(cb)),
            pl.BlockSpec((1, 1, ts, hg), lambda b, gi, c: (b, gi, c, 0)),
            pl.BlockSpec((1, 1, hg, ts), lambda b, gi, c: (b, gi, 0, c)),
            pl.BlockSpec((1, 1, hg), lambda b, gi, c: (gi, 0, 0)),
            pl.BlockSpec((1, hg, 1), lambda b, gi, c: (gi, 0, 0)),
            pl.BlockSpec((1, 1, gw), lambda b, gi, c: (gi, 0, 0)),
            pl.BlockSpec((1, 1, gw), lambda b, gi, c: (gi, 0, 0)),
            pl.BlockSpec((CONV_WIDTH, gw), par),
            pl.BlockSpec((CONV_WIDTH, n), par),
            pl.BlockSpec((CONV_WIDTH, n), par),
            pl.BlockSpec((1, gw), par),
            pl.BlockSpec((1, n), par),
            pl.BlockSpec((1, n), par),
        ],
        out_specs=pl.BlockSpec((1, ts, gw), lambda b, gi, c: (b, c, gi)),
        out_shape=jax.ShapeDtypeStruct((bsz, seq, wid), BF16),
        scratch_shapes=[
            pltpu.VMEM((hg // 2, n, 2 * SSM_HEAD_DIM), F32),
            pltpu.VMEM((8, gw), F32), pltpu.VMEM((8, n), F32), pltpu.VMEM((8, n), F32),
            pltpu.VMEM((ts + 8, gw), F32), pltpu.VMEM((ts + 8, n), F32), pltpu.VMEM((ts + 8, n), F32),
        ],
        compiler_params=_params(("parallel", "parallel", "arbitrary")),
        name="ssd",
    )(zx, zx, zx, zx, dtg, dtt, a_log.reshape(g, 1, hg), a_log.reshape(g, hg, 1),
      jnp.repeat(d_skip, SSM_HEAD_DIM).reshape(g, 1, gw), g_norm.reshape(g, 1, gw),
      cw_x, cw_b, cw_c, cb_x, cb_b, cb_c)


def _merge_body(u_ref, ya_ref, ys_ref, wg0_ref, wg1_ref, wa_ref, ws_ref, o_ref):
    u = u_ref[...]
    g0 = jax.nn.sigmoid(jnp.dot(u, wg0_ref[...], preferred_element_type=F32))
    g1 = jax.nn.sigmoid(jnp.dot(u, wg1_ref[...], preferred_element_type=F32))
    a = jnp.dot(ya_ref[...], wa_ref[...], preferred_element_type=F32)
    s = jnp.dot(ys_ref[...], ws_ref[...], preferred_element_type=F32)
    o_ref[...] = (g0 * a + g1 * s).astype(o_ref.dtype)


def _merge(u, ya, ys, w_gate, w_a, w_s):
    m, d = u.shape
    tm, tn = MERGE_TM, MERGE_TN
    nj = d // tn
    return pl.pallas_call(
        _merge_body,
        grid=(m // tm, nj),
        in_specs=[
            pl.BlockSpec((tm, d), lambda i, j: (i, 0)),
            pl.BlockSpec((tm, ya.shape[1]), lambda i, j: (i, 0)),
            pl.BlockSpec((tm, ys.shape[1]), lambda i, j: (i, 0)),
            pl.BlockSpec((d, tn), lambda i, j: (0, j)),
            pl.BlockSpec((d, tn), lambda i, j: (0, nj + j)),
            pl.BlockSpec((w_a.shape[0], tn), lambda i, j: (0, j)),
            pl.BlockSpec((w_s.shape[0], tn), lambda i, j: (0, j)),
        ],
        out_specs=pl.BlockSpec((tm, tn), lambda i, j: (i, j)),
        out_shape=jax.ShapeDtypeStruct((m, d), BF16),
        compiler_params=_params(("parallel", "arbitrary")),
        name="merge",
    )(u, ya, ys, w_gate, w_gate, w_a, w_s)


def _out_body(mg_ref, x_ref, w_ref, g_ref, h_ref, u_ref):
    h = x_ref[...] + jnp.dot(mg_ref[...], w_ref[...], preferred_element_type=F32)
    h_ref[...] = h
    y = h * lax.rsqrt(jnp.mean(h * h, axis=-1, keepdims=True) + NORM_EPS)
    u_ref[...] = (y * g_ref[...]).astype(u_ref.dtype)


def _out_proj(merged, x2, w_out, g_ffn):
    m, d = x2.shape
    tm = OUT_TM
    row = pl.BlockSpec((tm, d), lambda i: (i, 0))
    return pl.pallas_call(
        _out_body,
        grid=(m // tm,),
        in_specs=[row, row, pl.BlockSpec((d, d), lambda i: (0, 0)), pl.BlockSpec((1, d), lambda i: (0, 0))],
        out_specs=[row, row],
        out_shape=[jax.ShapeDtypeStruct((m, d), F32), jax.ShapeDtypeStruct((m, d), BF16)],
        compiler_params=_params(("parallel",)),
        name="out_proj",
    )(merged, x2, w_out, g_ffn.reshape(1, d))


def _ffn_up_body(u_ref, wg_ref, wu_ref, o_ref):
    u = u_ref[...]
    gate = jnp.dot(u, wg_ref[...], preferred_element_type=F32)
    up = jnp.dot(u, wu_ref[...], preferred_element_type=F32)
    o_ref[...] = (_silu(gate) * up).astype(o_ref.dtype)


def _ffn_up(u2, w_in):
    m, d = u2.shape
    dff = w_in.shape[1] // 2
    tm, tn = FFN_TM, FFN_TN
    nj = dff // tn
    return pl.pallas_call(
        _ffn_up_body,
        grid=(m // tm, nj),
        in_specs=[pl.BlockSpec((tm, d), lambda i, j: (i, 0)),
                  pl.BlockSpec((d, tn), lambda i, j: (0, j)),
                  pl.BlockSpec((d, tn), lambda i, j: (0, nj + j))],
        out_specs=pl.BlockSpec((tm, tn), lambda i, j: (i, j)),
        out_shape=jax.ShapeDtypeStruct((m, dff), BF16),
        compiler_params=_params(("parallel", "arbitrary")),
        name="ffn_up",
    )(u2, w_in, w_in)


def _ffn_down_body(a_ref, w_ref, h_ref, g_ref, o_ref):
    h = h_ref[...] + jnp.dot(a_ref[...], w_ref[...], preferred_element_type=F32)
    y = h * lax.rsqrt(jnp.mean(h * h, axis=-1, keepdims=True) + NORM_EPS)
    o_ref[...] = y * g_ref[...]


def _ffn_down(act, w_out, h1, g_final):
    m, dff = act.shape
    d = w_out.shape[1]
    tm = DOWN_TM
    return pl.pallas_call(
        _ffn_down_body,
        grid=(m // tm,),
        in_specs=[pl.BlockSpec((tm, dff), lambda i: (i, 0)),
                  pl.BlockSpec((dff, d), lambda i: (0, 0), pipeline_mode=pl.Buffered(1)),
                  pl.BlockSpec((tm, d), lambda i: (i, 0)),
                  pl.BlockSpec((1, d), lambda i: (0, 0))],
        out_specs=pl.BlockSpec((tm, d), lambda i: (i, 0)),
        out_shape=jax.ShapeDtypeStruct((m, d), F32),
        compiler_params=_params(("parallel",)),
        name="ffn_down",
    )(act, w_out, h1, g_final.reshape(1, d))


def _layer(h, w_in, w_gate, w_attn_branch, w_ssm_branch, w_out, conv_w, conv_b, dt_bias, a_log, d_skip,
           g_ssm_norm, g_mix, g_ffn, w_ffn_in, w_ffn_out, g_last):
    bsz, seq, d = h.shape
    m = bsz * seq
    aw = N_HEADS * HEAD_DIM
    kw = N_KV_HEADS * HEAD_DIM
    iw = N_IDX_HEADS * IDX_DIM
    sw = SSM_EXPAND * d
    heads = sw // SSM_HEAD_DIM
    bcw = SSM_GROUPS * SSM_STATE
    cuts = [0]
    for s in (aw, kw, kw, iw, IDX_DIM, N_IDX_HEADS, sw, sw + 2 * bcw, heads):
        cuts.append(cuts[-1] + s)
    wb = w_in.astype(BF16)
    seg = lambda i, j=None: wb[:, cuts[i]:cuts[i + 1 if j is None else j]]
    pad = lambda a, n: jnp.pad(a, ((0, 0), (0, n - a.shape[1])))
    w_small = jnp.concatenate([seg(4), seg(4), pad(seg(5), LANES), pad(seg(8), LANES)], axis=1)

    x2 = h.reshape(m, d)
    u = _rmsnorm(x2, g_mix)
    u3 = u.reshape(bsz, seq, d)

    tm = min(PROJ_TM, seq)
    cos_a, sin_a = _rope_tables(seq, HEAD_DIM)
    cos_i, sin_i = _rope_tables(seq, IDX_DIM)
    tab = _table_spec(tm)
    blk = lambda n: pl.BlockSpec((1, tm, n), lambda b, i, j: (b, i, j))
    blk_t = lambda n: pl.BlockSpec((1, n, tm), lambda b, i, j: (b, j, i))

    tn = min(PROJ_TN, aw)
    qt = _proj_call(u3, seg(0), functools.partial(_rope_t_epilogue, HEAD_DIM, HEAD_DIM ** -0.5 * math.log2(math.e)),
                    (cos_a, sin_a), (tab, tab), jax.ShapeDtypeStruct((bsz, aw, seq), BF16), blk_t(tn), tm, tn,
                    "proj_q")
    k, vt = _proj_call(u3, seg(1, 3), functools.partial(_kv_epilogue, kw), (cos_a, sin_a), (tab, tab),
                       [jax.ShapeDtypeStruct((bsz, seq, kw), BF16), jax.ShapeDtypeStruct((bsz, kw, seq), BF16)],
                       [pl.BlockSpec((1, tm, kw), lambda b, i, j: (b, i, 0)),
                        pl.BlockSpec((1, kw, tm), lambda b, i, j: (b, 0, i))], tm, 2 * kw, "proj_kv")
    tn = min(PROJ_TN, iw)
    qit = _proj_call(u3, seg(3), functools.partial(_rope_t_epilogue, IDX_DIM, 1.0),
                     (cos_i, sin_i), (tab, tab), jax.ShapeDtypeStruct((bsz, iw, seq), BF16), blk_t(tn), tm, tn,
                     "proj_qidx")
    k2, wt, dt = _proj_call(
        u3, w_small, functools.partial(_small_epilogue, N_IDX_HEADS, heads),
        (cos_i, sin_i, dt_bias.reshape(1, heads)),
        (tab, tab, pl.BlockSpec((1, heads), lambda b, i, j: (0, 0))),
        [jax.ShapeDtypeStruct((bsz, seq, 2 * IDX_DIM), BF16), jax.ShapeDtypeStruct((bsz, N_IDX_HEADS, seq), F32),
         jax.ShapeDtypeStruct((bsz, seq, heads), F32)],
        [pl.BlockSpec((1, tm, 2 * IDX_DIM), lambda b, i, j: (b, i, 0)),
         pl.BlockSpec((1, N_IDX_HEADS, tm), lambda b, i, j: (b, 0, i)),
         pl.BlockSpec((1, tm, heads), lambda b, i, j: (b, i, 0))], tm, 3 * LANES, "proj_small")
    nzx = 2 * sw + 2 * bcw
    tn = min(PROJ_TN, nzx)
    zx = _proj_call(u3, seg(6, 8), _plain_epilogue, (), (), jax.ShapeDtypeStruct((bsz, seq, nzx), F32),
                    blk(tn), tm, tn, "proj_zx")

    y_attn = _dsa(qt, qit, wt, k2, k, vt, min(TOPK_MAX, seq // 4))
    y_ssm = _ssd(zx, dt, conv_w, conv_b, a_log, d_skip, g_ssm_norm)

    merged = _merge(u, y_attn.reshape(m, aw), y_ssm.reshape(m, sw), w_gate.astype(BF16),
                    w_attn_branch.astype(BF16), w_ssm_branch.astype(BF16))
    h1, u2 = _out_proj(merged, x2, w_out.astype(BF16), g_ffn)
    act = _ffn_up(u2, w_ffn_in.astype(BF16))
    return _ffn_down(act, w_ffn_out.astype(BF16), h1, g_last).reshape(bsz, seq, d)


def kernel(x, w_in, w_gate, w_attn_branch, w_ssm_branch, w_out, conv_w, conv_b, dt_bias, a_log, d_skip, g_ssm_norm,
           g_mix, g_ffn, w_ffn_in, w_ffn_out, g_final):
    depth = w_in.shape[0]
    assert depth == 1, "the final norm is fused into the last layer's down projection"
    return _layer(x, w_in[0], w_gate[0], w_attn_branch[0], w_ssm_branch[0], w_out[0], conv_w[0], conv_b[0],
                  dt_bias[0], a_log[0], d_skip[0], g_ssm_norm[0], g_mix[0], g_ffn[0], w_ffn_in[0], w_ffn_out[0],
                  g_final)
```
